```python
import jax, jax.numpy as jnp
from jax import lax
import numpy as np

D_MODEL = 2048
BATCH = 4
SEQ = 4096
DEPTH = 1
DEC_BATCH = 32
DEC_SEQ = 4
PAST_LEN = 16384
PAGE_SIZE = 128

D_SB = D_MODEL // 2
SB_HEADS = 16
SB_HEAD_DIM = D_SB // SB_HEADS
D_CONV = D_MODEL - D_SB
CONV_WIDTH = 31
Q_BLOCK = 128
SB_BIAS_INIT = -8.0
N_EXPERTS = 64
TOP_K = 6
N_EXPERT_GROUPS = 8
TOPK_GROUPS = 4
D_EXPERT = 512
D_SHARED = 512
ROUTED_SCALE = 2.5
EXPERT_ROWS = 128
RMS_EPS = 1e-6
LN_EPS = 1e-5

kernel_name = 'hymba_stickbreak_conformer_moe_step'


def rmsnorm(x, g):
    xf = x.astype(jnp.float32)
    y = xf * lax.rsqrt(jnp.mean(xf * xf, axis=-1, keepdims=True) + RMS_EPS)
    return (y * g.astype(jnp.float32)).astype(x.dtype)


def layernorm(x, g, b):
    xf = x.astype(jnp.float32)
    mu = jnp.mean(xf, axis=-1, keepdims=True)
    var = jnp.mean(jnp.square(xf - mu), axis=-1, keepdims=True)
    return ((xf - mu) * lax.rsqrt(var + LN_EPS) * g.astype(jnp.float32) + b.astype(jnp.float32)).astype(x.dtype)


def adaln(c, w, b, n):
    return jnp.split(jax.nn.silu(c) @ w + b, n, axis=-1)


def modulate(h, shift, scale):
    return h * (1 + scale[:, None, :]) + shift[:, None, :]


def project_in(h, w_in):
    b, t, _ = h.shape
    proj = jnp.einsum('btd,de->bte', h, w_in)
    q, k, v, a, gt = jnp.split(proj, [D_SB, 2 * D_SB, 3 * D_SB, 3 * D_SB + D_CONV], axis=-1)
    heads = lambda z: z.reshape(b, t, SB_HEADS, SB_HEAD_DIM)
    return heads(q), heads(k), heads(v), a * jax.nn.sigmoid(gt)


def stick_breaking(q, k, v, bias, q_start):
    nq, nk = q.shape[1], k.shape[1]
    z = (jnp.einsum('bqhd,bkhd->bhqk', q, k).astype(jnp.float32) * (SB_HEAD_DIM ** -0.5)
         + bias.astype(jnp.float32)[None, :, None, None])
    q_pos = q_start + jnp.arange(nq)
    mask = jnp.arange(nk)[None, :] < q_pos[:, None]
    log_keep = jnp.where(mask, jax.nn.log_sigmoid(-z), 0.0)
    later = lax.cumsum(log_keep, axis=3, reverse=True) - log_keep
    w = jnp.where(mask, jnp.exp(jax.nn.log_sigmoid(z) + later), 0.0)
    return jnp.einsum('bhqk,bkhd->bqhd', w.astype(v.dtype), v)


def conformer_conv(u_ext, conv_w, conv_b, ln_g, ln_b):
    y = lax.conv_general_dilated(u_ext, conv_w[:, None, :].astype(u_ext.dtype), (1,), 'VALID',
                                 dimension_numbers=('NWC', 'WIO', 'NWC'), feature_group_count=D_CONV)
    return jax.nn.silu(layernorm(y + conv_b, ln_g, ln_b))


def merge_groups(o_sb, o_cv, g_sb_out, w_o):
    b, t = o_sb.shape[:2]
    o = jnp.concatenate([rmsnorm(o_sb.reshape(b, t, D_SB), g_sb_out), o_cv], axis=-1)
    return jnp.einsum('bte,ed->btd', o, w_o)


def swiglu(h, w_gate, w_up, w_down):
    return (jax.nn.silu(h @ w_gate) * (h @ w_up)) @ w_down


def route(h, w_router, b_router):
    t = h.shape[0]
    scores = jax.nn.sigmoid((h @ w_router).astype(jnp.float32))
    biased = scores + b_router.astype(jnp.float32)
    grouped = biased.reshape(t, N_EXPERT_GROUPS, N_EXPERTS // N_EXPERT_GROUPS)
    group_score = jnp.sum(lax.top_k(grouped, 2)[0], axis=-1)
    top_groups = lax.top_k(group_score, TOPK_GROUPS)[1]
    group_mask = jnp.any(top_groups[:, :, None] == jnp.arange(N_EXPERT_GROUPS)[None, None, :], axis=1)
    expert_mask = jnp.repeat(group_mask, N_EXPERTS // N_EXPERT_GROUPS, axis=1)
    idx = lax.top_k(jnp.where(expert_mask, biased, -jnp.inf), TOP_K)[1]
    gate = jnp.take_along_axis(scores, idx, axis=1)
    gate = gate / jnp.sum(gate, axis=-1, keepdims=True) * ROUTED_SCALE
    return idx, gate


def routed_experts(h, idx, gate, w_gate, w_up, w_down):
    t, d = h.shape
    n = t * TOP_K
    flat_e = idx.reshape(n)
    flat_tok = jnp.repeat(jnp.arange(t, dtype=jnp.int32), TOP_K)
    flat_gate = gate.reshape(n)
    order = jnp.argsort(flat_e)
    e_sorted = flat_e[order]
    counts = jnp.bincount(flat_e, length=N_EXPERTS)
    blocks_per_e = (counts + EXPERT_ROWS - 1) // EXPERT_ROWS
    block_end = jnp.cumsum(blocks_per_e)
    row_start = jnp.cumsum(counts) - counts
    dest = (block_end - blocks_per_e)[e_sorted] * EXPERT_ROWS + jnp.arange(n) - row_start[e_sorted]
    n_blocks = (n + N_EXPERTS * (EXPERT_ROWS - 1)) // EXPERT_ROWS
    tok_buf = jnp.zeros((n_blocks * EXPERT_ROWS,), jnp.int32).at[dest].set(flat_tok[order])
    gate_buf = jnp.zeros((n_blocks * EXPERT_ROWS,), jnp.float32).at[dest].set(flat_gate[order])
    block_expert = jnp.minimum(jnp.searchsorted(block_end, jnp.arange(n_blocks), side='right'), N_EXPERTS - 1)

    def expert_block(acc, blk):
        tok, g, e = blk
        y = swiglu(h[tok], w_gate[e], w_up[e], w_down[e]).astype(jnp.float32)
        return acc.at[tok].add(y * g[:, None]), None

    acc, _ = lax.scan(expert_block, jnp.zeros((t, d), jnp.float32),
                      (tok_buf.reshape(n_blocks, EXPERT_ROWS), gate_buf.reshape(n_blocks, EXPERT_ROWS), block_expert))
    return acc.astype(h.dtype)


def moe_ffn(h, w_router, b_router, w_exp_gate, w_exp_up, w_exp_down, w_sh_gate, w_sh_up, w_sh_down):
    idx, gate = route(h, w_router, b_router)
    return routed_experts(h, idx, gate, w_exp_gate, w_exp_up, w_exp_down) + swiglu(h, w_sh_gate, w_sh_up, w_sh_down)


def setup_inputs(seed: int = 0) -> dict:
    key = jax.random.key(seed)
    ks = jax.random.split(key, 32)
    n_pages = PAST_LEN // PAGE_SIZE
    n_used = DEC_BATCH * n_pages
    n_pool = n_used + max(1, n_used // 4)
    d_in = 3 * D_SB + 2 * D_CONV
    nrm = lambda k, shape, s: s * jax.random.normal(k, shape, jnp.float32)
    page_table = jax.random.permutation(ks[5], n_pool)[:n_used].reshape(DEC_BATCH, n_pages).astype(jnp.int32)
    return {
        'x_prompt': nrm(ks[0], (BATCH, SEQ, D_MODEL), 1.0),
        'x_sample': nrm(ks[1], (DEC_BATCH, DEC_SEQ, D_MODEL), 1.0),
        'cache_k': nrm(ks[2], (DEPTH, n_pool, PAGE_SIZE, SB_HEADS, SB_HEAD_DIM), 1.0),
        'cache_v': nrm(ks[3], (DEPTH, n_pool, PAGE_SIZE, SB_HEADS, SB_HEAD_DIM), 1.0),
        'state_conv': nrm(ks[4], (DEPTH, DEC_BATCH, CONV_WIDTH - 1, D_CONV), 0.5),
        'page_table': page_table,
        'c_prompt': nrm(ks[6], (BATCH, D_MODEL), 1.0),
        'c_sample': nrm(ks[7], (DEC_BATCH, D_MODEL), 1.0),
        'w_ada': nrm(ks[8], (DEPTH, D_MODEL, 6 * D_MODEL), 0.5 * D_MODEL ** -0.5),
        'b_ada': nrm(ks[9], (DEPTH, 6 * D_MODEL), 0.01),
        'g_mix': 1.0 + nrm(ks[10], (DEPTH, D_MODEL), 0.1),
        'w_in': nrm(ks[11], (DEPTH, D_MODEL, d_in), D_MODEL ** -0.5),
        'b_sb': SB_BIAS_INIT + nrm(ks[30], (DEPTH, SB_HEADS), 0.5),
        'conv_w': nrm(ks[12], (DEPTH, CONV_WIDTH, D_CONV), CONV_WIDTH ** -0.5),
        'conv_b': nrm(ks[13], (DEPTH, D_CONV), 0.01),
        'conv_ln_g': 1.0 + nrm(ks[14], (DEPTH, D_CONV), 0.1),
        'conv_ln_b': nrm(ks[15], (DEPTH, D_CONV), 0.01),
        'g_sb_out': 1.0 + nrm(ks[16], (DEPTH, D_SB), 0.1),
        'w_o': nrm(ks[17], (DEPTH, D_SB + D_CONV, D_MODEL), (D_SB + D_CONV) ** -0.5),
        'g_ffn': 1.0 + nrm(ks[18], (DEPTH, D_MODEL), 0.1),
        'w_router': nrm(ks[19], (DEPTH, D_MODEL, N_EXPERTS), D_MODEL ** -0.5),
        'b_router': nrm(ks[20], (DEPTH, N_EXPERTS), 0.01),
        'w_exp_gate': nrm(ks[21], (DEPTH, N_EXPERTS, D_MODEL, D_EXPERT), D_MODEL ** -0.5),
        'w_exp_up': nrm(ks[22], (DEPTH, N_EXPERTS, D_MODEL, D_EXPERT), D_MODEL ** -0.5),
        'w_exp_down': nrm(ks[23], (DEPTH, N_EXPERTS, D_EXPERT, D_MODEL), D_EXPERT ** -0.5),
        'w_sh_gate': nrm(ks[24], (DEPTH, D_MODEL, D_SHARED), D_MODEL ** -0.5),
        'w_sh_up': nrm(ks[25], (DEPTH, D_MODEL, D_SHARED), D_MODEL ** -0.5),
        'w_sh_down': nrm(ks[26], (DEPTH, D_SHARED, D_MODEL), D_SHARED ** -0.5),
        'w_ada_out': nrm(ks[27], (D_MODEL, 2 * D_MODEL), 0.5 * D_MODEL ** -0.5),
        'b_ada_out': nrm(ks[28], (2 * D_MODEL,), 0.01),
        'g_final': 1.0 + nrm(ks[29], (D_MODEL,), 0.1),
    }


def reference(x_prompt, x_sample, cache_k, cache_v, state_conv, page_table, c_prompt, c_sample,
              w_ada, b_ada, g_mix, w_in, b_sb, conv_w, conv_b, conv_ln_g, conv_ln_b, g_sb_out, w_o,
              g_ffn, w_router, b_router, w_exp_gate, w_exp_up, w_exp_down, w_sh_gate, w_sh_up,
              w_sh_down, w_ada_out, b_ada_out, g_final):
    bsz, seq, d = x_prompt.shape
    dbsz, dseq, _ = x_sample.shape
    past = page_table.shape[1] * cache_k.shape[2]
    n_prompt = bsz * seq
    xp, xs = x_prompt, x_sample
    k_p, v_p, cv_p, k_s, v_s, cv_s = [], [], [], [], [], []
    for layer in range(DEPTH):
        shm_p, scm_p, gtm_p, shf_p, scf_p, gtf_p = adaln(c_prompt, w_ada[layer], b_ada[layer], 6)
        shm_s, scm_s, gtm_s, shf_s, scf_s, gtf_s = adaln(c_sample, w_ada[layer], b_ada[layer], 6)
        conv_args = (conv_w[layer], conv_b[layer], conv_ln_g[layer], conv_ln_b[layer])

        q, k, v, u = project_in(modulate(rmsnorm(xp, g_mix[layer]), shm_p, scm_p), w_in[layer])
        o_sb = jnp.concatenate(
            [stick_breaking(q[:, i * Q_BLOCK:(i + 1) * Q_BLOCK], k[:, :(i + 1) * Q_BLOCK],
                            v[:, :(i + 1) * Q_BLOCK], b_sb[layer], i * Q_BLOCK) for i in range(seq // Q_BLOCK)], axis=1)
        u_ext = jnp.pad(u, ((0, 0), (CONV_WIDTH - 1, 0), (0, 0)))
        o_cv = conformer_conv(u_ext, *conv_args)
        xp = xp + gtm_p[:, None, :] * merge_groups(o_sb, o_cv, g_sb_out[layer], w_o[layer])
        k_p.append(k)
        v_p.append(v)
        cv_p.append(u_ext[:, -(CONV_WIDTH - 1):])

        q, k, v, u = project_in(modulate(rmsnorm(xs, g_mix[layer]), shm_s, scm_s), w_in[layer])
        k_all = jnp.concatenate([cache_k[layer][page_table].reshape(dbsz, past, SB_HEADS, SB_HEAD_DIM).astype(k.dtype), k], axis=1)
        v_all = jnp.concatenate([cache_v[layer][page_table].reshape(dbsz, past, SB_HEADS, SB_HEAD_DIM).astype(v.dtype), v], axis=1)
        o_sb = stick_breaking(q, k_all, v_all, b_sb[layer], past)
        u_ext = jnp.concatenate([state_conv[layer].astype(u.dtype), u], axis=1)
        o_cv = conformer_conv(u_ext, *conv_args)
        xs = xs + gtm_s[:, None, :] * merge_groups(o_sb, o_cv, g_sb_out[layer], w_o[layer])
        k_s.append(k)
        v_s.append(v)
        cv_s.append(u_ext[:, -(CONV_WIDTH - 1):])

        h2 = jnp.concatenate([modulate(rmsnorm(xp, g_ffn[layer]), shf_p, scf_p).reshape(n_prompt, d),
                              modulate(rmsnorm(xs, g_ffn[layer]), shf_s, scf_s).reshape(dbsz * dseq, d)], axis=0)
        f = moe_ffn(h2, w_router[layer], b_router[layer], w_exp_gate[layer], w_exp_up[layer], w_exp_down[layer],
                    w_sh_gate[layer], w_sh_up[layer], w_sh_down[layer])
        xp = xp + gtf_p[:, None, :] * f[:n_prompt].reshape(bsz, seq, d)
        xs = xs + gtf_s[:, None, :] * f[n_prompt:].reshape(dbsz, dseq, d)

    sh_p, sc_p = adaln(c_prompt, w_ada_out, b_ada_out, 2)
    sh_s, sc_s = adaln(c_sample, w_ada_out, b_ada_out, 2)
    y_prompt = modulate(rmsnorm(xp, g_final), sh_p, sc_p)
    y_sample = modulate(rmsnorm(xs, g_final), sh_s, sc_s)
    return (y_prompt, y_sample, jnp.stack(k_p), jnp.stack(v_p), jnp.stack(cv_p),
            jnp.stack(k_s), jnp.stack(v_s), jnp.stack(cv_s))
```

```python
import functools

import jax
import jax.numpy as jnp
from jax import lax
from jax.experimental import pallas as pl
from jax.experimental.pallas import tpu as pltpu

F32 = jnp.float32
BF16 = jnp.bfloat16
I32 = jnp.int32

TOP_K = 6
N_EXPERT_GROUPS = 8
TOPK_GROUPS = 4
ROUTED_SCALE = 2.5
RMS_EPS = 1e-6
LN_EPS = 1e-5

LANES = 128
SUBLANES = 8
VMEM_BYTES_V7X = 64 * 1024 * 1024

ATTN_TILE = 256
PAGES_PER_STEP = 8
EXPERT_ROWS = 256
CONV_HALO = 32
ROW_CHUNKS = 16


def _params(sem, vmem_mib):
    return pltpu.CompilerParams(dimension_semantics=sem, vmem_limit_bytes=vmem_mib * 1024 * 1024)


def _row(ref):
    return ref[0] if len(ref.shape) == 3 else ref[...]


def _sigmoid(x):
    return 1.0 / (1.0 + jnp.exp(-x))


def _softplus(z):
    return jnp.maximum(z, 0.0) + jnp.log(1.0 + jnp.exp(-jnp.abs(z)))


def _rms(x):
    return x * lax.rsqrt(jnp.mean(x * x, axis=-1, keepdims=True) + RMS_EPS)


def _mod_spec(per_seq, tm, d, chunk, tokens_per_seq):
    if per_seq:
        return pl.BlockSpec((1, 1, d), lambda i, *_: ((i * tm) // tokens_per_seq, 0, chunk))
    return pl.BlockSpec((tm, d), lambda i, *_: (i, chunk))


def _ada_kernel(c_ref, w_ref, b_ref, o_ref):
    c = c_ref[...]
    s = (c * _sigmoid(c)).astype(BF16)
    o_ref[...] = jnp.dot(s, w_ref[...].astype(BF16), preferred_element_type=F32) + b_ref[...]


def _ada(c, w, b, tn=512):
    m, d = c.shape
    n = w.shape[1]
    return pl.pallas_call(
        _ada_kernel,
        grid=(n // tn,),
        in_specs=[pl.BlockSpec((m, d), lambda j: (0, 0)),
                  pl.BlockSpec((d, tn), lambda j: (0, j)),
                  pl.BlockSpec((1, tn), lambda j: (0, j))],
        out_specs=pl.BlockSpec((m, tn), lambda j: (0, j)),
        out_shape=jax.ShapeDtypeStruct((m, n), F32),
        name="ada", compiler_params=_params(("arbitrary",), 32),
    )(c, w, b.reshape(1, n))


def _inproj_kernel(x_ref, sh_ref, sc_ref, g_ref, w_ref, q_ref, k_ref, v_ref, u_ref, h_scr, a_scr, *, q_scale):
    j = pl.program_id(1)

    @pl.when(j == 0)
    def _():
        h = (_rms(x_ref[...]) * g_ref[...]) * (1.0 + _row(sc_ref)) + _row(sh_ref)
        h_scr[...] = h.astype(BF16)

    p = jnp.dot(h_scr[...], w_ref[...], preferred_element_type=F32)

    @pl.when(j == 0)
    def _():
        q_ref[...] = (p * q_scale).astype(BF16)

    @pl.when(j == 1)
    def _():
        k_ref[...] = p

    @pl.when(j == 2)
    def _():
        v_ref[...] = p

    @pl.when(j == 3)
    def _():
        a_scr[...] = p

    @pl.when(j == 4)
    def _():
        u_ref[...] = a_scr[...] * _sigmoid(p)


def _inproj(x, mod, per_seq, tokens_per_seq, g_mix, w_in_bf16, tm, q_scale):
    t, d = x.shape
    gw = w_in_bf16.shape[1] // 5
    tok = lambda i, j: (i, 0)
    out = jax.ShapeDtypeStruct((t, gw), F32)
    return pl.pallas_call(
        functools.partial(_inproj_kernel, q_scale=q_scale),
        grid=(t // tm, 5),
        in_specs=[pl.BlockSpec((tm, d), tok),
                  _mod_spec(per_seq, tm, d, 0, tokens_per_seq),
                  _mod_spec(per_seq, tm, d, 1, tokens_per_seq),
                  pl.BlockSpec((1, d), lambda i, j: (0, 0)),
                  pl.BlockSpec((d, gw), lambda i, j: (0, j))],
        out_specs=[pl.BlockSpec((tm, gw), tok)] * 4,
        out_shape=[jax.ShapeDtypeStruct((t, gw), BF16), out, out, out],
        scratch_shapes=[pltpu.VMEM((tm, d), BF16), pltpu.VMEM((tm, gw), F32)],
        name="inproj", compiler_params=_params(("arbitrary", "arbitrary"), 48),
    )(x, mod, mod, g_mix.reshape(1, d), w_in_bf16)


def _attn_kernel(bias_ref, q_ref, k_ref, v_ref, o_ref, kb, qm, vm, *, head_dim, tile):
    hp = pl.program_id(1)
    seq = q_ref.shape[0]
    lane = lax.broadcasted_iota(I32, (1, LANES), 1)
    first = lane < head_dim
    q = q_ref[...]
    v = v_ref[...].astype(BF16)
    zero = jnp.zeros_like(q)
    kb[...] = k_ref[...].astype(BF16)
    qm[0] = jnp.where(first, q, zero)
    qm[1] = jnp.where(first, zero, q)
    vm[0] = jnp.where(first, v, zero)
    vm[1] = jnp.where(first, zero, v)

    r = lax.broadcasted_iota(I32, (tile, tile), 0)
    c = lax.broadcasted_iota(I32, (tile, tile), 1)
    neg_later = jnp.where(r > c, -1.0, 0.0).astype(BF16)
    causal = c < r

    def one_tile(h, bias, qt, k0, masked, acc, carry):
        kt = kb[pl.ds(k0, tile), :]
        z = lax.dot_general(qt, kt, (((1,), (1,)), ((), ())), preferred_element_type=F32) + bias
        sp = _softplus(z)
        if masked:
            sp = jnp.where(causal, sp, 0.0)
        later = jnp.dot(sp.astype(BF16), neg_later, preferred_element_type=F32) + carry
        w = jnp.exp(z - sp + later)
        if masked:
            w = jnp.where(causal, w, 0.0)
        acc = acc + jnp.dot(w.astype(BF16), vm[h, pl.ds(k0, tile), :], preferred_element_type=F32)
        carry = carry - jnp.sum(sp, axis=1, keepdims=True)
        return acc, carry

    def q_block(i, _):
        q0 = pl.multiple_of(i * tile, tile)
        out = jnp.zeros((tile, LANES), F32)
        for h in range(2):
            bias = bias_ref[2 * hp + h]
            qt = qm[h, pl.ds(q0, tile), :]
            acc, carry = one_tile(h, bias, qt, q0, True, jnp.zeros((tile, LANES), F32),
                                  jnp.zeros((tile, 1), F32))

            def k_block(jj, st, h=h, bias=bias, qt=qt):
                k0 = pl.multiple_of((i - 1 - jj) * tile, tile)
                return one_tile(h, bias, qt, k0, False, *st)

            acc, _ = lax.fori_loop(0, i, k_block, (acc, carry))
            out = out + acc
        o_ref[pl.ds(q0, tile), :] = out
        return 0

    lax.fori_loop(0, seq // tile, q_block, 0)


def _prompt_attention(q, k, v, b_sb, bsz, seq, head_dim):
    t, dsb = q.shape
    blk = pl.BlockSpec((seq, LANES), lambda b, hp: (b, hp))
    return pl.pallas_call(
        functools.partial(_attn_kernel, head_dim=head_dim, tile=ATTN_TILE),
        grid=(bsz, dsb // LANES),
        in_specs=[pl.BlockSpec(memory_space=pltpu.SMEM), blk, blk, blk],
        out_specs=blk,
        out_shape=jax.ShapeDtypeStruct((t, dsb), F32),
        scratch_shapes=[pltpu.VMEM((seq, LANES), BF16), pltpu.VMEM((2, seq, LANES), BF16),
                        pltpu.VMEM((2, seq, LANES), BF16)],
        name="prompt_attn", compiler_params=_params(("arbitrary", "arbitrary"), 48),
    )(b_sb, q, k, v)


def _sattn_kernel(pt_ref, qbd_ref, bias_ref, kn_ref, vn_ref, *rest, n_q, head_dim, page):
    npg = PAGES_PER_STEP
    k_refs, v_refs = rest[:npg], rest[npg:2 * npg]
    o_ref, acc_scr, carry_scr = rest[2 * npg:]
    step = pl.program_id(1)
    rows = qbd_ref.shape[1]
    qbd = qbd_ref[0]
    bias = bias_ref[...]
    r = lax.broadcasted_iota(I32, (page, page), 0)
    c = lax.broadcasted_iota(I32, (page, page), 1)
    neg_later = jnp.where(r > c, -1.0, 0.0).astype(BF16)

    def block(kp, vp, mask, acc, carry):
        z = jnp.dot(qbd, kp.astype(BF16), preferred_element_type=F32) + bias
        sp = _softplus(z)
        if mask is not None:
            sp = jnp.where(mask, sp, 0.0)
        later = jnp.dot(sp.astype(BF16), neg_later, preferred_element_type=F32) + carry
        w = jnp.exp(z - sp + later)
        if mask is not None:
            w = jnp.where(mask, w, 0.0)
        acc = acc + lax.dot_general(w.astype(BF16), vp.astype(BF16), (((1,), (1,)), ((), ())),
                                    preferred_element_type=F32)
        return acc, carry - jnp.sum(sp, axis=1, keepdims=True)

    @pl.when(step == 0)
    def _():
        qi = lax.broadcasted_iota(I32, (rows, page), 0) % n_q
        kj = lax.broadcasted_iota(I32, (rows, page), 1)
        acc, carry = block(kn_ref[0], vn_ref[0], kj < qi, jnp.zeros(acc_scr.shape, F32),
                           jnp.zeros((rows, 1), F32))
        acc_scr[...] = acc
        carry_scr[...] = carry

    acc = acc_scr[...]
    carry = carry_scr[...]
    for j in reversed(range(npg)):
        acc, carry = block(k_refs[j][0], v_refs[j][0], None, acc, carry)
    acc_scr[...] = acc
    carry_scr[...] = carry

    @pl.when(step == pl.num_programs(1) - 1)
    def _():
        width = acc_scr.shape[1]
        rh = lax.broadcasted_iota(I32, (rows, width), 0) // n_q
        ch = lax.broadcasted_iota(I32, (rows, width), 1) // head_dim
        own = jnp.where(rh == ch, acc, 0.0)
        folded = own[:, 0:LANES]
        for b in range(1, width // LANES):
            folded = folded + own[:, b * LANES:(b + 1) * LANES]
        o_ref[0] = folded


def _sample_attention(qbd, bias_col, k_new, v_new, cache_k, cache_v, page_table, n_q, head_dim):
    dbsz, rows, dsb = qbd.shape
    n_pages = page_table.shape[1]
    page = cache_k.shape[2]
    steps = n_pages // PAGES_PER_STEP

    def page_spec(j):
        return pl.BlockSpec((1, dsb, page),
                            lambda b, s, pt: (pt[b, (steps - 1 - s) * PAGES_PER_STEP + j], 0, 0))

    per_b = lambda b, s, pt: (b, 0, 0)
    grid_spec = pltpu.PrefetchScalarGridSpec(
        num_scalar_prefetch=1,
        grid=(dbsz, steps),
        in_specs=[pl.BlockSpec((1, rows, dsb), per_b),
                  pl.BlockSpec((rows, 1), lambda b, s, pt: (0, 0)),
                  pl.BlockSpec((1, dsb, page), per_b),
                  pl.BlockSpec((1, dsb, page), per_b)]
                 + [page_spec(j) for j in range(PAGES_PER_STEP)] * 2,
        out_specs=pl.BlockSpec((1, rows, LANES), per_b),
        scratch_shapes=[pltpu.VMEM((rows, dsb), F32), pltpu.VMEM((rows, 1), F32)],
    )
    return pl.pallas_call(
        functools.partial(_sattn_kernel, n_q=n_q, head_dim=head_dim, page=page),
        grid_spec=grid_spec,
        out_shape=jax.ShapeDtypeStruct((dbsz, rows, LANES), F32),
        name="sample_attn", compiler_params=_params(("arbitrary", "arbitrary"), 48),
    )(page_table, qbd, bias_col, k_new, v_new, *([cache_k] * PAGES_PER_STEP), *([cache_v] * PAGES_PER_STEP))


def _conv_kernel(halo_ref, u_ref, cw_ref, cb_ref, lg_ref, lb_ref, o_ref, ext, y_scr, *, width, zero_first, sub):
    tt, ch = u_ref.shape[1], u_ref.shape[2]
    halo = halo_ref[0]
    if zero_first:
        halo = jnp.where(pl.program_id(1) == 0, 0.0, halo)
    ext[0:CONV_HALO, :] = halo
    ext[CONV_HALO:CONV_HALO + tt, :] = u_ref[0]
    lead = CONV_HALO - (width - 1)

    def lane_block(cb, _):
        c0 = pl.multiple_of(cb * LANES, LANES)
        for ts in range(tt // sub):
            acc = jnp.zeros((sub, LANES), F32)
            for w in range(width):
                acc = acc + (ext[pl.ds(lead + w + ts * sub, sub), pl.ds(c0, LANES)]
                             * cw_ref[pl.ds(w, 1), pl.ds(c0, LANES)])
            y_scr[pl.ds(ts * sub, sub), pl.ds(c0, LANES)] = acc
        return 0

    lax.fori_loop(0, ch // LANES, lane_block, 0)
    y = y_scr[...] + cb_ref[...]
    mu = jnp.mean(y, axis=-1, keepdims=True)
    yc = y - mu
    var = jnp.mean(yc * yc, axis=-1, keepdims=True)
    yn = yc * lax.rsqrt(var + LN_EPS) * lg_ref[...] + lb_ref[...]
    o_ref[0] = (yn * _sigmoid(yn)).astype(BF16)


def _conv(halo_src, u3, conv_w, conv_b, ln_g, ln_b, tt, halo_from_u):
    b, t, ch = u3.shape
    width = conv_w.shape[0]
    if halo_from_u:
        halo_spec = pl.BlockSpec((1, CONV_HALO, ch),
                                 lambda bi, i: (bi, jnp.maximum(i * (tt // CONV_HALO) - 1, 0), 0))
    else:
        halo_spec = pl.BlockSpec((1, CONV_HALO, ch), lambda bi, i: (bi, 0, 0))
    vec = pl.BlockSpec((1, ch), lambda bi, i: (0, 0))
    return pl.pallas_call(
        functools.partial(_conv_kernel, width=width, zero_first=halo_from_u, sub=min(tt, 64)),
        grid=(b, t // tt),
        in_specs=[halo_spec,
                  pl.BlockSpec((1, tt, ch), lambda bi, i: (bi, i, 0)),
                  pl.BlockSpec((width, ch), lambda bi, i: (0, 0)),
                  vec, vec, vec],
        out_specs=pl.BlockSpec((1, tt, ch), lambda bi, i: (bi, i, 0)),
        out_shape=jax.ShapeDtypeStruct((b, t, ch), BF16),
        scratch_shapes=[pltpu.VMEM((CONV_HALO + tt, ch), F32), pltpu.VMEM((tt, ch), F32)],
        name="conv", compiler_params=_params(("arbitrary", "arbitrary"), 32),
    )(halo_src, u3, conv_w, conv_b.reshape(1, ch), ln_g.reshape(1, ch), ln_b.reshape(1, ch))


def _outproj_kernel(osb_ref, ocv_ref, x_ref, gsb_ref, wsb_ref, wcv_ref, gate_ref, gffn_ref, shf_ref, scf_ref,
                    xo_ref, h2_ref):
    osb = (_rms(osb_ref[...]) * gsb_ref[...]).astype(BF16)
    y = jnp.dot(osb, wsb_ref[...], preferred_element_type=F32)
    y = y + jnp.dot(ocv_ref[...], wcv_ref[...], preferred_element_type=F32)
    xn = x_ref[...] + _row(gate_ref) * y
    xo_ref[...] = xn
    h2 = (_rms(xn) * gffn_ref[...]) * (1.0 + _row(scf_ref)) + _row(shf_ref)
    h2_ref[...] = h2.astype(BF16)


def _outproj(o_sb, o_cv, x, g_sb_out, w_sb, w_cv, mod, per_seq, tokens_per_seq, g_ffn, tm):
    t, d = x.shape
    dsb, dcv = o_sb.shape[1], o_cv.shape[1]
    tok = lambda i: (i, 0)
    fix = lambda i: (0, 0)
    return pl.pallas_call(
        _outproj_kernel,
        grid=(t // tm,),
        in_specs=[pl.BlockSpec((tm, dsb), tok), pl.BlockSpec((tm, dcv), tok), pl.BlockSpec((tm, d), tok),
                  pl.BlockSpec((1, dsb), fix), pl.BlockSpec((dsb, d), fix), pl.BlockSpec((dcv, d), fix),
                  _mod_spec(per_seq, tm, d, 2, tokens_per_seq),
                  pl.BlockSpec((1, d), fix),
                  _mod_spec(per_seq, tm, d, 3, tokens_per_seq),
                  _mod_spec(per_seq, tm, d, 4, tokens_per_seq)],
        out_specs=[pl.BlockSpec((tm, d), tok), pl.BlockSpec((tm, d), tok)],
        out_shape=[jax.ShapeDtypeStruct((t, d), F32), jax.ShapeDtypeStruct((t, d), BF16)],
        name="outproj", compiler_params=_params(("arbitrary",), 48),
    )(o_sb, o_cv, x, g_sb_out.reshape(1, dsb), w_sb, w_cv, mod, g_ffn.reshape(1, d), mod, mod)


def _route_kernel(h_ref, whi_ref, wlo_ref, br_ref, cin_ref, idx_ref, gate_ref, pos_ref, cnt_ref):
    n_e = whi_ref.shape[0]
    tm = h_ref.shape[0]
    gsz = n_e // N_EXPERT_GROUPS
    neg = -jnp.inf

    @pl.when(pl.program_id(0) == 0)
    def _():
        cnt_ref[...] = cin_ref[...]

    h = h_ref[...]
    nt = (((1,), (1,)), ((), ()))
    logits = (lax.dot_general(whi_ref[...], h, nt, preferred_element_type=F32)
              + lax.dot_general(wlo_ref[...], h, nt, preferred_element_type=F32))
    scores = _sigmoid(logits)
    biased = scores + br_ref[...]

    sub = lax.broadcasted_iota(I32, (gsz, tm), 0).astype(F32)

    def top1(x, iota, size):
        m = jnp.max(x, axis=0, keepdims=True)
        first = jnp.min(jnp.where(x == m, iota, float(size)), axis=0, keepdims=True)
        return m, iota == first

    gscore = []
    for g in range(N_EXPERT_GROUPS):
        xg = biased[g * gsz:(g + 1) * gsz, :]
        m1, hit = top1(xg, sub, gsz)
        m2 = jnp.max(jnp.where(hit, neg, xg), axis=0, keepdims=True)
        gscore.append(m1 + m2)

    chosen = [jnp.zeros((1, tm), jnp.bool_) for _ in range(N_EXPERT_GROUPS)]
    for _ in range(TOPK_GROUPS):
        m = functools.reduce(jnp.maximum, gscore)
        found = jnp.zeros((1, tm), jnp.bool_)
        for g in range(N_EXPERT_GROUPS):
            take = jnp.logical_and(gscore[g] == m, jnp.logical_not(found))
            found = jnp.logical_or(found, take)
            chosen[g] = jnp.logical_or(chosen[g], take)
            gscore[g] = jnp.where(take, neg, gscore[g])

    masked = jnp.concatenate(
        [jnp.where(chosen[g], biased[g * gsz:(g + 1) * gsz, :], neg) for g in range(N_EXPERT_GROUPS)], axis=0)

    eio = lax.broadcasted_iota(I32, (n_e, tm), 0).astype(F32)
    hits, raw = [], []
    for _ in range(TOP_K):
        _, hit = top1(masked, eio, n_e)
        hits.append(hit)
        raw.append(jnp.sum(jnp.where(hit, scores, 0.0), axis=0, keepdims=True))
        masked = jnp.where(hit, neg, masked)
    denom = functools.reduce(jnp.add, raw)

    sel = functools.reduce(jnp.logical_or, hits)
    sel_f = jnp.where(sel, 1.0, 0.0)
    tr = lax.broadcasted_iota(I32, (tm, tm), 0)
    tc = lax.broadcasted_iota(I32, (tm, tm), 1)
    before = jnp.where(tr < tc, 1.0, 0.0).astype(BF16)
    slot = jnp.dot(sel_f.astype(BF16), before, preferred_element_type=F32) + cnt_ref[:, 0:1]

    idx_ref[...] = jnp.zeros(idx_ref.shape, I32)
    gate_ref[...] = jnp.zeros(gate_ref.shape, F32)
    pos_ref[...] = jnp.zeros(pos_ref.shape, I32)
    for k in range(TOP_K):
        idx_ref[k:k + 1, :] = jnp.sum(jnp.where(hits[k], eio, 0.0), axis=0, keepdims=True).astype(I32)
        gate_ref[k:k + 1, :] = raw[k] / denom * ROUTED_SCALE
        pos_ref[k:k + 1, :] = jnp.sum(jnp.where(hits[k], slot, 0.0), axis=0, keepdims=True).astype(I32)
    cnt_ref[...] = cnt_ref[...] + jnp.sum(sel_f, axis=1, keepdims=True)


def _route(h2, wr_hi, wr_lo, b_router, counts_in, tm):
    t, d = h2.shape
    n_e = wr_hi.shape[0]
    fix = lambda i: (0, 0)
    per_tok = pl.BlockSpec((SUBLANES, tm), lambda i: (0, i))
    return pl.pallas_call(
        _route_kernel,
        grid=(t // tm,),
        in_specs=[pl.BlockSpec((tm, d), lambda i: (i, 0)),
                  pl.BlockSpec((n_e, d), fix), pl.BlockSpec((n_e, d), fix),
                  pl.BlockSpec((n_e, 1), fix), pl.BlockSpec((n_e, LANES), fix)],
        out_specs=[per_tok, per_tok, per_tok, pl.BlockSpec((n_e, LANES), fix)],
        out_shape=[jax.ShapeDtypeStruct((SUBLANES, t), I32), jax.ShapeDtypeStruct((SUBLANES, t), F32),
                   jax.ShapeDtypeStruct((SUBLANES, t), I32), jax.ShapeDtypeStruct((n_e, LANES), F32)],
        name="route", compiler_params=_params(("arbitrary",), 32),
    )(h2, wr_hi, wr_lo, b_router.reshape(n_e, 1), counts_in)


def _row_copy(src, dst, sem, src_row, dst_row):
    return pltpu.make_async_copy(src.at[pl.ds(src_row * ROW_CHUNKS, ROW_CHUNKS), :],
                                 dst.at[pl.ds(dst_row * ROW_CHUNKS, ROW_CHUNKS), :], sem)


def _dispatch_kernel(dest_ref, h_ref, xs_in_ref, xs_ref, rows, sem):
    del xs_in_ref
    tm = h_ref.shape[0]
    h = h_ref[...].astype(F32)
    for cidx in range(ROW_CHUNKS):
        rows[pl.ds(cidx, tm, stride=ROW_CHUNKS), :] = h[:, cidx * LANES:(cidx + 1) * LANES]

    def start(t, _):
        for k in range(TOP_K):
            _row_copy(rows, xs_ref, sem, t, dest_ref[k, t]).start()
        return 0

    def wait(t, _):
        for k in range(TOP_K):
            _row_copy(rows, xs_ref, sem, t, dest_ref[k, t]).wait()
        return 0

    lax.fori_loop(0, tm, start, 0)
    lax.fori_loop(0, tm, wait, 0)


def _dispatch(dest, h2, xs, tm):
    t, d = h2.shape
    return pl.pallas_call(
        _dispatch_kernel,
        grid=(t // tm,),
        in_specs=[pl.BlockSpec((SUBLANES, tm), lambda i: (0, i), memory_space=pltpu.SMEM),
                  pl.BlockSpec((tm, d), lambda i: (i, 0)),
                  pl.BlockSpec(memory_space=pl.ANY)],
        out_specs=pl.BlockSpec(memory_space=pl.ANY),
        out_shape=jax.ShapeDtypeStruct(xs.shape, xs.dtype),
        scratch_shapes=[pltpu.VMEM((tm * ROW_CHUNKS, LANES), F32), pltpu.SemaphoreType.DMA(())],
        input_output_aliases={2: 0},
        name="dispatch", compiler_params=_params(("arbitrary",), 32),
    )(dest, h2, xs)


def _expert_kernel(be_ref, xs_ref, wg_ref, wu_ref, wd_ref, y_ref):
    del be_ref
    rb = xs_ref.shape[0] // ROW_CHUNKS
    x = jnp.concatenate([xs_ref[pl.ds(cidx, rb, stride=ROW_CHUNKS), :] for cidx in range(ROW_CHUNKS)],
                        axis=1).astype(BF16)
    g = jnp.dot(x, wg_ref[0], preferred_element_type=F32)
    u = jnp.dot(x, wu_ref[0], preferred_element_type=F32)
    a = ((g * _sigmoid(g)) * u).astype(BF16)
    y = jnp.dot(a, wd_ref[0], preferred_element_type=F32)
    for cidx in range(ROW_CHUNKS):
        y_ref[pl.ds(cidx, rb, stride=ROW_CHUNKS), :] = y[:, cidx * LANES:(cidx + 1) * LANES]


def _experts(block_expert, xs, w_gate, w_up, w_down):
    n_blocks = block_expert.shape[0]
    _, d, de = w_gate.shape
    rows = EXPERT_ROWS * ROW_CHUNKS
    grid_spec = pltpu.PrefetchScalarGridSpec(
        num_scalar_prefetch=1,
        grid=(n_blocks,),
        in_specs=[pl.BlockSpec((rows, LANES), lambda i, be: (i, 0)),
                  pl.BlockSpec((1, d, de), lambda i, be: (be[i], 0, 0)),
                  pl.BlockSpec((1, d, de), lambda i, be: (be[i], 0, 0)),
                  pl.BlockSpec((1, de, d), lambda i, be: (be[i], 0, 0))],
        out_specs=pl.BlockSpec((rows, LANES), lambda i, be: (i, 0)),
    )
    return pl.pallas_call(
        _expert_kernel,
        grid_spec=grid_spec,
        out_shape=jax.ShapeDtypeStruct(xs.shape, F32),
        name="experts", compiler_params=_params(("arbitrary",), 48),
    )(block_expert, xs, w_gate, w_up, w_down)


def _combine_kernel(dest_ref, gates_ref, h_ref, x_ref, wg_ref, wu_ref, wd_ref, gt_ref, gfin_ref, sh_ref, sc_ref,
                    y_hbm, o_ref, buf, sem):
    tm = h_ref.shape[0]

    def copy(t, k):
        return pltpu.make_async_copy(
            y_hbm.at[pl.ds(dest_ref[k, t] * ROW_CHUNKS, ROW_CHUNKS), :],
            buf.at[k, pl.ds(t * ROW_CHUNKS, ROW_CHUNKS), :], sem)

    def start(t, _):
        for k in range(TOP_K):
            copy(t, k).start()
        return 0

    def wait(t, _):
        for k in range(TOP_K):
            copy(t, k).wait()
        return 0

    lax.fori_loop(0, tm, start, 0)

    h = h_ref[...]
    g = jnp.dot(h, wg_ref[...], preferred_element_type=F32)
    u = jnp.dot(h, wu_ref[...], preferred_element_type=F32)
    shared = jnp.dot(((g * _sigmoid(g)) * u).astype(BF16), wd_ref[...], preferred_element_type=F32)

    lax.fori_loop(0, tm, wait, 0)

    gates = gates_ref[...]
    routed = jnp.zeros(shared.shape, F32)
    for k in range(TOP_K):
        yk = jnp.concatenate([buf[k, pl.ds(cidx, tm, stride=ROW_CHUNKS), :] for cidx in range(ROW_CHUNKS)], axis=1)
        routed = routed + yk * gates[:, k:k + 1]
    xo = x_ref[...] + _row(gt_ref) * (routed + shared)
    o_ref[...] = (_rms(xo) * gfin_ref[...]) * (1.0 + _row(sc_ref)) + _row(sh_ref)


def _combine(dest, gates_tok, h2, x, w_sg, w_su, w_sd, mod, mod_out, per_seq, tokens_per_seq, g_final, y_rows, tm):
    t, d = x.shape
    ds = w_sg.shape[1]
    tok = lambda i: (i, 0)
    fix = lambda i: (0, 0)
    return pl.pallas_call(
        _combine_kernel,
        grid=(t // tm,),
        in_specs=[pl.BlockSpec((SUBLANES, tm), lambda i: (0, i), memory_space=pltpu.SMEM),
                  pl.BlockSpec((tm, SUBLANES), tok),
                  pl.BlockSpec((tm, d), tok), pl.BlockSpec((tm, d), tok),
                  pl.BlockSpec((d, ds), fix), pl.BlockSpec((d, ds), fix), pl.BlockSpec((ds, d), fix),
                  _mod_spec(per_seq, tm, d, 5, tokens_per_seq),
                  pl.BlockSpec((1, d), fix),
                  _mod_spec(per_seq, tm, d, 0, tokens_per_seq),
                  _mod_spec(per_seq, tm, d, 1, tokens_per_seq),
                  pl.BlockSpec(memory_space=pl.ANY)],
        out_specs=pl.BlockSpec((tm, d), tok),
        out_shape=jax.ShapeDtypeStruct((t, d), F32),
        scratch_shapes=[pltpu.VMEM((TOP_K, tm * ROW_CHUNKS, LANES), F32), pltpu.SemaphoreType.DMA(())],
        name="combine", compiler_params=_params(("arbitrary",), 48),
    )(dest, gates_tok, h2, x, w_sg, w_su, w_sd, mod, g_final.reshape(1, d), mod_out, mod_out, y_rows)


def kernel(x_prompt, x_sample, cache_k, cache_v, state_conv, page_table, c_prompt, c_sample, w_ada, b_ada, g_mix, w_in, b_sb, conv_w, conv_b, conv_ln_g, conv_ln_b, g_sb_out, w_o, g_ffn, w_router, b_router, w_exp_gate, w_exp_up, w_exp_down, w_sh_gate, w_sh_up, w_sh_down, w_ada_out, b_ada_out, g_final):
    bsz, seq, d = x_prompt.shape
    dbsz, dseq, _ = x_sample.shape
    depth, n_pool, page, heads, head_dim = cache_k.shape
    assert depth == 1 and d == ROW_CHUNKS * LANES
    dsb = heads * head_dim
    dcv = conv_w.shape[-1]
    width = conv_w.shape[1]
    n_e = w_router.shape[-1]
    tp, ts = bsz * seq, dbsz * dseq
    assert dsb == dcv and w_in.shape[-1] == 5 * dsb and width - 1 <= CONV_HALO

    c_all = jnp.concatenate([c_prompt, c_sample], axis=0)
    pad = (-c_all.shape[0]) % SUBLANES
    c_all = jnp.pad(c_all, ((0, pad), (0, 0)))
    mod = _ada(c_all, w_ada[0], b_ada[0])
    mod_out = _ada(c_all, w_ada_out, b_ada_out)
    mod_p = mod[:bsz].reshape(bsz, 1, 6 * d)
    modo_p = mod_out[:bsz].reshape(bsz, 1, 2 * d)
    mod_s = jnp.repeat(mod[bsz:bsz + dbsz], dseq, axis=0)
    modo_s = jnp.repeat(mod_out[bsz:bsz + dbsz], dseq, axis=0)

    xp = x_prompt.reshape(tp, d)
    xs = x_sample.reshape(ts, d)
    w_in_b = w_in[0].astype(BF16)
    q_scale = head_dim ** -0.5

    q_p, k_p, v_p, u_p = _inproj(xp, mod_p, True, seq, g_mix[0], w_in_b, 512, q_scale)
    o_sb_p = _prompt_attention(q_p, k_p, v_p, b_sb[0], bsz, seq, head_dim)
    u_p3 = u_p.reshape(bsz, seq, dcv)
    o_cv_p = _conv(u_p3, u_p3, conv_w[0], conv_b[0], conv_ln_g[0], conv_ln_b[0], 256, True).reshape(tp, dcv)
    w_o_b = w_o[0].astype(BF16)
    x1_p, h2_p = _outproj(o_sb_p, o_cv_p, xp, g_sb_out[0], w_o_b[:dsb], w_o_b[dsb:], mod_p, True, seq, g_ffn[0], 256)

    q_s, k_s, v_s, u_s = _inproj(xs, mod_s, False, 1, g_mix[0], w_in_b, ts, q_scale)
    rows = heads * dseq
    eye = jnp.eye(heads, dtype=q_s.dtype)
    qbd = jnp.einsum('bqhd,hg->bhqgd', q_s.reshape(dbsz, dseq, heads, head_dim), eye).reshape(dbsz, rows, dsb)
    bias_col = jnp.repeat(b_sb[0], dseq).reshape(rows, 1)
    new_page = lambda z: jnp.pad(z.reshape(dbsz, dseq, dsb).transpose(0, 2, 1), ((0, 0), (0, 0), (0, page - dseq)))
    pages = lambda cache: cache[0].transpose(0, 2, 3, 1).reshape(n_pool, dsb, page)
    folded = _sample_attention(qbd, bias_col, new_page(k_s), new_page(v_s), pages(cache_k), pages(cache_v),
                               page_table, dseq, head_dim)
    folded = folded.reshape(dbsz, heads, dseq, 2, head_dim)
    o_sb_s = (folded[:, :, :, 0] + folded[:, :, :, 1]).transpose(0, 2, 1, 3).reshape(ts, dsb)
    u_s3 = u_s.reshape(dbsz, dseq, dcv)
    u_ext_s = jnp.concatenate([state_conv[0], u_s3], axis=1)
    halo_s = jnp.pad(state_conv[0], ((0, 0), (CONV_HALO - (width - 1), 0), (0, 0)))
    rows_s = 2 * SUBLANES
    u_s16 = jnp.pad(u_s3, ((0, 0), (0, rows_s - dseq), (0, 0)))
    o_cv_s = _conv(halo_s, u_s16, conv_w[0], conv_b[0], conv_ln_g[0], conv_ln_b[0], rows_s, False)
    o_cv_s = o_cv_s[:, :dseq].reshape(ts, dcv)
    x1_s, h2_s = _outproj(o_sb_s, o_cv_s, xs, g_sb_out[0], w_o_b[:dsb], w_o_b[dsb:], mod_s, False, 1, g_ffn[0], ts)

    wr_t = w_router[0].T
    wr_hi = wr_t.astype(BF16)
    wr_lo = (wr_t - wr_hi.astype(F32)).astype(BF16)
    zero_counts = jnp.zeros((n_e, LANES), F32)
    idx_p, gate_p, pos_p, counts = _route(h2_p, wr_hi, wr_lo, b_router[0], zero_counts, 512)
    idx_s, gate_s, pos_s, counts = _route(h2_s, wr_hi, wr_lo, b_router[0], counts, ts)

    n_assign = (tp + ts) * TOP_K
    n_blocks = (n_assign + n_e * (EXPERT_ROWS - 1)) // EXPERT_ROWS
    cnt = counts[:, 0].astype(I32)
    blocks_per_e = (cnt + EXPERT_ROWS - 1) // EXPERT_ROWS
    block_end = jnp.cumsum(blocks_per_e)
    row_base = (block_end - blocks_per_e) * EXPERT_ROWS
    block_expert = jnp.minimum(jnp.searchsorted(block_end, jnp.arange(n_blocks), side='right'), n_e - 1).astype(I32)
    experts = jnp.arange(n_e, dtype=I32)

    def slots(idx, pos):
        return pos + jnp.sum(jnp.where(idx[:, :, None] == experts, row_base, 0), axis=-1).astype(I32)

    dest_p, dest_s = slots(idx_p, pos_p), slots(idx_s, pos_s)

    xs_rows = jnp.zeros((n_blocks * EXPERT_ROWS * ROW_CHUNKS, LANES), F32)
    xs_rows = _dispatch(dest_p, h2_p, xs_rows, 256)
    xs_rows = _dispatch(dest_s, h2_s, xs_rows, ts)
    y_rows = _experts(block_expert, xs_rows, w_exp_gate[0].astype(BF16), w_exp_up[0].astype(BF16),
                      w_exp_down[0].astype(BF16))

    w_sg, w_su, w_sd = w_sh_gate[0].astype(BF16), w_sh_up[0].astype(BF16), w_sh_down[0].astype(BF16)
    y_p = _combine(dest_p, gate_p.T, h2_p, x1_p, w_sg, w_su, w_sd, mod_p, modo_p, True, seq, g_final, y_rows, 128)
    y_s = _combine(dest_s, gate_s.T, h2_s, x1_s, w_sg, w_su, w_sd, mod_s, modo_s, False, 1, g_final, y_rows, ts)

    kv = lambda z, b, t: z.reshape(1, b, t, heads, head_dim)
    return (y_p.reshape(bsz, seq, d), y_s.reshape(dbsz, dseq, d),
            kv(k_p, bsz, seq), kv(v_p, bsz, seq), u_p3[:, seq - (width - 1):][None],
            kv(k_s, dbsz, dseq), kv(v_s, dbsz, dseq), u_ext_s[:, dseq:][None])
```

```python
import functools

import jax
import jax.numpy as jnp
from jax import lax
from jax.experimental import pallas as pl
from jax.experimental.pallas import tpu as pltpu

F32 = jnp.float32
BF16 = jnp.bfloat16
I32 = jnp.int32

TOP_K = 6
N_EXPERT_GROUPS = 8
TOPK_GROUPS = 4
ROUTED_SCALE = 2.5
RMS_EPS = 1e-6
LN_EPS = 1e-5

LANES = 128
SUBLANES = 8
VMEM_BYTES_V7X = 64 * 1024 * 1024

ATTN_TILE = 256
PAGES_PER_STEP = 8
EXPERT_ROWS = 256
CONV_HALO = 32
ROW_CHUNKS = 16
ROW_PITCH = 20


def _params(sem, vmem_mib):
    return pltpu.CompilerParams(dimension_semantics=sem, vmem_limit_bytes=vmem_mib * 1024 * 1024)


def _row(ref):
    return ref[0] if len(ref.shape) == 3 else ref[...]


def _sigmoid(x):
    return 1.0 / (1.0 + jnp.exp(-x))


def _softplus(z):
    return jnp.maximum(z, 0.0) + jnp.log(1.0 + jnp.exp(-jnp.abs(z)))


def _rms(x):
    return x * lax.rsqrt(jnp.mean(x * x, axis=-1, keepdims=True) + RMS_EPS)


def _mod_spec(per_seq, tm, d, chunk, tokens_per_seq):
    if per_seq:
        return pl.BlockSpec((1, 1, d), lambda i, *_: ((i * tm) // tokens_per_seq, 0, chunk))
    return pl.BlockSpec((tm, d), lambda i, *_: (i, chunk))


def _ada_kernel(c_ref, w_ref, b_ref, o_ref):
    c = c_ref[...]
    s = (c * _sigmoid(c)).astype(BF16)
    o_ref[...] = jnp.dot(s, w_ref[...].astype(BF16), preferred_element_type=F32) + b_ref[...]


def _ada(c, w, b, tn=512):
    m, d = c.shape
    n = w.shape[1]
    return pl.pallas_call(
        _ada_kernel,
        grid=(n // tn,),
        in_specs=[pl.BlockSpec((m, d), lambda j: (0, 0)),
                  pl.BlockSpec((d, tn), lambda j: (0, j)),
                  pl.BlockSpec((1, tn), lambda j: (0, j))],
        out_specs=pl.BlockSpec((m, tn), lambda j: (0, j)),
        out_shape=jax.ShapeDtypeStruct((m, n), F32),
        name="ada", compiler_params=_params(("arbitrary",), 32),
    )(c, w, b.reshape(1, n))


def _inproj_kernel(x_ref, sh_ref, sc_ref, g_ref, w_ref, q_ref, k_ref, v_ref, u_ref, h_scr, a_scr, *, q_scale):
    j = pl.program_id(1)

    @pl.when(j == 0)
    def _():
        h = (_rms(x_ref[...]) * g_ref[...]) * (1.0 + _row(sc_ref)) + _row(sh_ref)
        h_scr[...] = h.astype(BF16)

    p = jnp.dot(h_scr[...], w_ref[...], preferred_element_type=F32)

    @pl.when(j == 0)
    def _():
        q_ref[...] = (p * q_scale).astype(BF16)

    @pl.when(j == 1)
    def _():
        k_ref[...] = p

    @pl.when(j == 2)
    def _():
        v_ref[...] = p

    @pl.when(j == 3)
    def _():
        a_scr[...] = p

    @pl.when(j == 4)
    def _():
        u_ref[...] = a_scr[...] * _sigmoid(p)


def _inproj(x, mod, per_seq, tokens_per_seq, g_mix, w_in_bf16, tm, q_scale):
    t, d = x.shape
    gw = w_in_bf16.shape[1] // 5
    tok = lambda i, j: (i, 0)
    out = jax.ShapeDtypeStruct((t, gw), F32)
    return pl.pallas_call(
        functools.partial(_inproj_kernel, q_scale=q_scale),
        grid=(t // tm, 5),
        in_specs=[pl.BlockSpec((tm, d), tok),
                  _mod_spec(per_seq, tm, d, 0, tokens_per_seq),
                  _mod_spec(per_seq, tm, d, 1, tokens_per_seq),
                  pl.BlockSpec((1, d), lambda i, j: (0, 0)),
                  pl.BlockSpec((d, gw), lambda i, j: (0, j))],
        out_specs=[pl.BlockSpec((tm, gw), tok)] * 4,
        out_shape=[jax.ShapeDtypeStruct((t, gw), BF16), out, out, out],
        scratch_shapes=[pltpu.VMEM((tm, d), BF16), pltpu.VMEM((tm, gw), F32)],
        name="inproj", compiler_params=_params(("arbitrary", "arbitrary"), 48),
    )(x, mod, mod, g_mix.reshape(1, d), w_in_bf16)


def _attn_kernel(bias_ref, q_ref, k_ref, v_ref, o_ref, kb, qm, vm, sp_scr, d_scr, *, head_dim, tile):
    hp = pl.program_id(1)
    seq = q_ref.shape[0]
    lane = lax.broadcasted_iota(I32, (1, LANES), 1)
    first = lane < head_dim
    q = q_ref[...]
    v = v_ref[...].astype(BF16)
    zero = jnp.zeros_like(q)
    kb[...] = k_ref[...].astype(BF16)
    qm[0] = jnp.where(first, q, zero)
    qm[1] = jnp.where(first, zero, q)
    vm[0] = jnp.where(first, v, zero)
    vm[1] = jnp.where(first, zero, v)

    r = lax.broadcasted_iota(I32, (tile, tile), 0)
    c = lax.broadcasted_iota(I32, (tile, tile), 1)
    neg_later = jnp.concatenate([jnp.where(r > c, -1.0, 0.0), jnp.full((tile, LANES), -1.0, F32)],
                                axis=1).astype(BF16)
    causal = c < r

    nt = (((1,), (1,)), ((), ()))
    heads = (0, 1)
    bias = [bias_ref[2 * hp + h] for h in heads]

    def scores(h, q0, k0):
        return lax.dot_general(qm[h, pl.ds(q0, tile), :], kb[pl.ds(k0, tile), :], nt,
                               preferred_element_type=F32) + bias[h]

    def keep_scores(h, slot, z, masked):
        sp = _softplus(z)
        d = z - sp
        if masked:
            sp = jnp.where(causal, sp, 0.0)
            d = jnp.where(causal, d, -jnp.inf)
        sp_scr[h, slot] = sp.astype(BF16)
        d_scr[h, slot] = d

    def later_sum(h, slot):
        return jnp.dot(sp_scr[h, slot], neg_later, preferred_element_type=F32)

    def weights(h, slot, k0, later, acc, carry):
        wide = jnp.concatenate([carry] * (tile // LANES), axis=1)
        w = jnp.exp(d_scr[h, slot] + (later[:, :tile] + wide))
        acc = acc + jnp.dot(w.astype(BF16), vm[h, pl.ds(k0, tile), :], preferred_element_type=F32)
        return acc, carry + later[:, tile:]

    def q_block(i, _):
        q0 = pl.multiple_of(i * tile, tile)
        for h in heads:
            keep_scores(h, 0, scores(h, q0, q0), True)

        def k_block(jj, st):
            slot = jj % 2
            k_prev = pl.multiple_of((i - jj) * tile, tile)
            k_next = pl.multiple_of((i - jj - 1) * tile, tile)
            later = [later_sum(h, slot) for h in heads]
            z = [scores(h, q0, k_next) for h in heads]
            st = [weights(h, slot, k_prev, later[h], *st[h]) for h in heads]
            for h in heads:
                keep_scores(h, 1 - slot, z[h], False)
            return tuple(st)

        init = tuple((jnp.zeros((tile, LANES), F32), jnp.zeros((tile, LANES), F32)) for _ in heads)
        st = lax.fori_loop(0, i, k_block, init)
        last = i % 2
        later = [later_sum(h, last) for h in heads]
        accs = [weights(h, last, 0, later[h], *st[h])[0] for h in heads]
        o_ref[pl.ds(q0, tile), :] = accs[0] + accs[1]
        return 0

    lax.fori_loop(0, seq // tile, q_block, 0)


def _prompt_attention(q, k, v, b_sb, bsz, seq, head_dim):
    t, dsb = q.shape
    blk = pl.BlockSpec((seq, LANES), lambda b, hp: (b, hp))
    return pl.pallas_call(
        functools.partial(_attn_kernel, head_dim=head_dim, tile=ATTN_TILE),
        grid=(bsz, dsb // LANES),
        in_specs=[pl.BlockSpec(memory_space=pltpu.SMEM), blk, blk, blk],
        out_specs=blk,
        out_shape=jax.ShapeDtypeStruct((t, dsb), F32),
        scratch_shapes=[pltpu.VMEM((seq, LANES), BF16), pltpu.VMEM((2, seq, LANES), BF16),
                        pltpu.VMEM((2, seq, LANES), BF16),
                        pltpu.VMEM((2, 2, ATTN_TILE, ATTN_TILE), BF16),
                        pltpu.VMEM((2, 2, ATTN_TILE, ATTN_TILE), F32)],
        name="prompt_attn", compiler_params=_params(("arbitrary", "arbitrary"), 48),
    )(b_sb, q, k, v)


def _sattn_kernel(pt_ref, qbd_ref, bias_ref, kn_ref, vn_ref, *rest, n_q, head_dim, page):
    npg = PAGES_PER_STEP
    k_refs, v_refs = rest[:npg], rest[npg:2 * npg]
    o_ref, acc_scr, carry_scr = rest[2 * npg:]
    step = pl.program_id(1)
    rows = qbd_ref.shape[1]
    qbd = qbd_ref[0]
    bias = bias_ref[...]
    r = lax.broadcasted_iota(I32, (page, page), 0)
    c = lax.broadcasted_iota(I32, (page, page), 1)
    neg_later = jnp.where(r > c, -1.0, 0.0).astype(BF16)

    def block(kp, vp, mask, acc, carry):
        z = jnp.dot(qbd, kp.astype(BF16), preferred_element_type=F32) + bias
        sp = _softplus(z)
        if mask is not None:
            sp = jnp.where(mask, sp, 0.0)
        later = jnp.dot(sp.astype(BF16), neg_later, preferred_element_type=F32) + carry
        w = jnp.exp(z - sp + later)
        if mask is not None:
            w = jnp.where(mask, w, 0.0)
        acc = acc + lax.dot_general(w.astype(BF16), vp.astype(BF16), (((1,), (1,)), ((), ())),
                                    preferred_element_type=F32)
        return acc, carry - jnp.sum(sp, axis=1, keepdims=True)

    @pl.when(step == 0)
    def _():
        qi = lax.broadcasted_iota(I32, (rows, page), 0) % n_q
        kj = lax.broadcasted_iota(I32, (rows, page), 1)
        acc, carry = block(kn_ref[0], vn_ref[0], kj < qi, jnp.zeros(acc_scr.shape, F32),
                           jnp.zeros((rows, 1), F32))
        acc_scr[...] = acc
        carry_scr[...] = carry

    order = list(reversed(range(npg)))
    zs = [jnp.dot(qbd, k_refs[j][0].astype(BF16), preferred_element_type=F32) + bias for j in order]
    sps = [_softplus(z) for z in zs]
    later_all = jnp.dot(jnp.concatenate([sp.astype(BF16) for sp in sps], axis=0), neg_later,
                        preferred_element_type=F32)
    carry = carry_scr[...]
    ws = []
    for n in range(npg):
        later = later_all[n * rows:(n + 1) * rows, :] + carry
        ws.append(jnp.exp(zs[n] - sps[n] + later).astype(BF16))
        carry = carry - jnp.sum(sps[n], axis=1, keepdims=True)
    acc = acc_scr[...]
    for n, j in enumerate(order):
        acc = acc + lax.dot_general(ws[n], v_refs[j][0].astype(BF16), (((1,), (1,)), ((), ())),
                                    preferred_element_type=F32)
    acc_scr[...] = acc
    carry_scr[...] = carry

    @pl.when(step == pl.num_programs(1) - 1)
    def _():
        width = acc_scr.shape[1]
        rh = lax.broadcasted_iota(I32, (rows, width), 0) // n_q
        ch = lax.broadcasted_iota(I32, (rows, width), 1) // head_dim
        own = jnp.where(rh == ch, acc, 0.0)
        folded = own[:, 0:LANES]
        for b in range(1, width // LANES):
            folded = folded + own[:, b * LANES:(b + 1) * LANES]
        o_ref[0] = folded


def _sample_attention(qbd, bias_col, k_new, v_new, cache_k, cache_v, page_table, n_q, head_dim):
    dbsz, rows, dsb = qbd.shape
    n_pages = page_table.shape[1]
    page = cache_k.shape[2]
    steps = n_pages // PAGES_PER_STEP

    def page_spec(j):
        return pl.BlockSpec((1, dsb, page),
                            lambda b, s, pt: (pt[b, (steps - 1 - s) * PAGES_PER_STEP + j], 0, 0))

    per_b = lambda b, s, pt: (b, 0, 0)
    grid_spec = pltpu.PrefetchScalarGridSpec(
        num_scalar_prefetch=1,
        grid=(dbsz, steps),
        in_specs=[pl.BlockSpec((1, rows, dsb), per_b),
                  pl.BlockSpec((rows, 1), lambda b, s, pt: (0, 0)),
                  pl.BlockSpec((1, dsb, page), per_b),
                  pl.BlockSpec((1, dsb, page), per_b)]
                 + [page_spec(j) for j in range(PAGES_PER_STEP)] * 2,
        out_specs=pl.BlockSpec((1, rows, LANES), per_b),
        scratch_shapes=[pltpu.VMEM((rows, dsb), F32), pltpu.VMEM((rows, 1), F32)],
    )
    return pl.pallas_call(
        functools.partial(_sattn_kernel, n_q=n_q, head_dim=head_dim, page=page),
        grid_spec=grid_spec,
        out_shape=jax.ShapeDtypeStruct((dbsz, rows, LANES), F32),
        name="sample_attn", compiler_params=_params(("arbitrary", "arbitrary"), 48),
    )(page_table, qbd, bias_col, k_new, v_new, *([cache_k] * PAGES_PER_STEP), *([cache_v] * PAGES_PER_STEP))


def _conv_kernel(halo_ref, u_ref, cw_ref, cb_ref, lg_ref, lb_ref, o_ref, ext, y_scr, *, width, zero_first, sub):
    tt, ch = u_ref.shape[1], u_ref.shape[2]
    halo = halo_ref[0]
    if zero_first:
        halo = jnp.where(pl.program_id(1) == 0, 0.0, halo)
    ext[0:CONV_HALO, :] = halo
    ext[CONV_HALO:CONV_HALO + tt, :] = u_ref[0]
    lead = CONV_HALO - (width - 1)

    def lane_block(cb, _):
        c0 = pl.multiple_of(cb * LANES, LANES)
        for ts in range(tt // sub):
            acc = jnp.zeros((sub, LANES), F32)
            for w in range(width):
                acc = acc + (ext[pl.ds(lead + w + ts * sub, sub), pl.ds(c0, LANES)]
                             * cw_ref[pl.ds(w, 1), pl.ds(c0, LANES)])
            y_scr[pl.ds(ts * sub, sub), pl.ds(c0, LANES)] = acc
        return 0

    lax.fori_loop(0, ch // LANES, lane_block, 0)
    y = y_scr[...] + cb_ref[...]
    mu = jnp.mean(y, axis=-1, keepdims=True)
    yc = y - mu
    var = jnp.mean(yc * yc, axis=-1, keepdims=True)
    yn = yc * lax.rsqrt(var + LN_EPS) * lg_ref[...] + lb_ref[...]
    o_ref[0] = (yn * _sigmoid(yn)).astype(BF16)


def _conv(halo_src, u3, conv_w, conv_b, ln_g, ln_b, tt, halo_from_u):
    b, t, ch = u3.shape
    width = conv_w.shape[0]
    if halo_from_u:
        halo_spec = pl.BlockSpec((1, CONV_HALO, ch),
                                 lambda bi, i: (bi, jnp.maximum(i * (tt // CONV_HALO) - 1, 0), 0))
    else:
        halo_spec = pl.BlockSpec((1, CONV_HALO, ch), lambda bi, i: (bi, 0, 0))
    vec = pl.BlockSpec((1, ch), lambda bi, i: (0, 0))
    return pl.pallas_call(
        functools.partial(_conv_kernel, width=width, zero_first=halo_from_u, sub=min(tt, 64)),
        grid=(b, t // tt),
        in_specs=[halo_spec,
                  pl.BlockSpec((1, tt, ch), lambda bi, i: (bi, i, 0)),
                  pl.BlockSpec((width, ch), lambda bi, i: (0, 0)),
                  vec, vec, vec],
        out_specs=pl.BlockSpec((1, tt, ch), lambda bi, i: (bi, i, 0)),
        out_shape=jax.ShapeDtypeStruct((b, t, ch), BF16),
        scratch_shapes=[pltpu.VMEM((CONV_HALO + tt, ch), F32), pltpu.VMEM((tt, ch), F32)],
        name="conv", compiler_params=_params(("arbitrary", "arbitrary"), 32),
    )(halo_src, u3, conv_w, conv_b.reshape(1, ch), ln_g.reshape(1, ch), ln_b.reshape(1, ch))


def _outproj_kernel(osb_ref, ocv_ref, x_ref, gsb_ref, wsb_ref, wcv_ref, gate_ref, gffn_ref, shf_ref, scf_ref,
                    xo_ref, h2_ref):
    osb = (_rms(osb_ref[...]) * gsb_ref[...]).astype(BF16)
    y = jnp.dot(osb, wsb_ref[...], preferred_element_type=F32)
    y = y + jnp.dot(ocv_ref[...], wcv_ref[...], preferred_element_type=F32)
    xn = x_ref[...] + _row(gate_ref) * y
    xo_ref[...] = xn
    h2 = (_rms(xn) * gffn_ref[...]) * (1.0 + _row(scf_ref)) + _row(shf_ref)
    h2_ref[...] = h2.astype(BF16)


def _outproj(o_sb, o_cv, x, g_sb_out, w_sb, w_cv, mod, per_seq, tokens_per_seq, g_ffn, tm):
    t, d = x.shape
    dsb, dcv = o_sb.shape[1], o_cv.shape[1]
    tok = lambda i: (i, 0)
    fix = lambda i: (0, 0)
    return pl.pallas_call(
        _outproj_kernel,
        grid=(t // tm,),
        in_specs=[pl.BlockSpec((tm, dsb), tok), pl.BlockSpec((tm, dcv), tok), pl.BlockSpec((tm, d), tok),
                  pl.BlockSpec((1, dsb), fix), pl.BlockSpec((dsb, d), fix), pl.BlockSpec((dcv, d), fix),
                  _mod_spec(per_seq, tm, d, 2, tokens_per_seq),
                  pl.BlockSpec((1, d), fix),
                  _mod_spec(per_seq, tm, d, 3, tokens_per_seq),
                  _mod_spec(per_seq, tm, d, 4, tokens_per_seq)],
        out_specs=[pl.BlockSpec((tm, d), tok), pl.BlockSpec((tm, d), tok)],
        out_shape=[jax.ShapeDtypeStruct((t, d), F32), jax.ShapeDtypeStruct((t, d), BF16)],
        name="outproj", compiler_params=_params(("arbitrary",), 48),
    )(o_sb, o_cv, x, g_sb_out.reshape(1, dsb), w_sb, w_cv, mod, g_ffn.reshape(1, d), mod, mod)


def _route_kernel(h_ref, whi_ref, wlo_ref, br_ref, cin_ref, idx_ref, gate_ref, pos_ref, cnt_ref):
    n_e = whi_ref.shape[0]
    tm = h_ref.shape[0]
    gsz = n_e // N_EXPERT_GROUPS
    neg = -jnp.inf

    @pl.when(pl.program_id(0) == 0)
    def _():
        cnt_ref[...] = cin_ref[...]

    h = h_ref[...]
    nt = (((1,), (1,)), ((), ()))
    logits = (lax.dot_general(whi_ref[...], h, nt, preferred_element_type=F32)
              + lax.dot_general(wlo_ref[...], h, nt, preferred_element_type=F32))
    scores = _sigmoid(logits)
    biased = scores + br_ref[...]

    sub = lax.broadcasted_iota(I32, (gsz, tm), 0).astype(F32)

    def top1(x, iota, size):
        m = jnp.max(x, axis=0, keepdims=True)
        first = jnp.min(jnp.where(x == m, iota, float(size)), axis=0, keepdims=True)
        return m, iota == first

    gscore = []
    for g in range(N_EXPERT_GROUPS):
        xg = biased[g * gsz:(g + 1) * gsz, :]
        m1, hit = top1(xg, sub, gsz)
        m2 = jnp.max(jnp.where(hit, neg, xg), axis=0, keepdims=True)
        gscore.append(m1 + m2)

    chosen = [jnp.zeros((1, tm), jnp.bool_) for _ in range(N_EXPERT_GROUPS)]
    for _ in range(TOPK_GROUPS):
        m = functools.reduce(jnp.maximum, gscore)
        found = jnp.zeros((1, tm), jnp.bool_)
        for g in range(N_EXPERT_GROUPS):
            take = jnp.logical_and(gscore[g] == m, jnp.logical_not(found))
            found = jnp.logical_or(found, take)
            chosen[g] = jnp.logical_or(chosen[g], take)
            gscore[g] = jnp.where(take, neg, gscore[g])

    masked = jnp.concatenate(
        [jnp.where(chosen[g], biased[g * gsz:(g + 1) * gsz, :], neg) for g in range(N_EXPERT_GROUPS)], axis=0)

    eio = lax.broadcasted_iota(I32, (n_e, tm), 0).astype(F32)
    hits, raw = [], []
    for _ in range(TOP_K):
        _, hit = top1(masked, eio, n_e)
        hits.append(hit)
        raw.append(jnp.sum(jnp.where(hit, scores, 0.0), axis=0, keepdims=True))
        masked = jnp.where(hit, neg, masked)
    denom = functools.reduce(jnp.add, raw)

    sel = functools.reduce(jnp.logical_or, hits)
    sel_f = jnp.where(sel, 1.0, 0.0)
    tr = lax.broadcasted_iota(I32, (tm, tm), 0)
    tc = lax.broadcasted_iota(I32, (tm, tm), 1)
    before = jnp.where(tr < tc, 1.0, 0.0).astype(BF16)
    slot = jnp.dot(sel_f.astype(BF16), before, preferred_element_type=F32) + cnt_ref[:, 0:1]

    idx_ref[...] = jnp.zeros(idx_ref.shape, I32)
    gate_ref[...] = jnp.zeros(gate_ref.shape, F32)
    pos_ref[...] = jnp.zeros(pos_ref.shape, I32)
    for k in range(TOP_K):
        idx_ref[k:k + 1, :] = jnp.sum(jnp.where(hits[k], eio, 0.0), axis=0, keepdims=True).astype(I32)
        gate_ref[k:k + 1, :] = raw[k] / denom * ROUTED_SCALE
        pos_ref[k:k + 1, :] = jnp.sum(jnp.where(hits[k], slot, 0.0), axis=0, keepdims=True).astype(I32)
    cnt_ref[...] = cnt_ref[...] + jnp.sum(sel_f, axis=1, keepdims=True)


def _route(h2, wr_hi, wr_lo, b_router, counts_in, tm):
    t, d = h2.shape
    n_e = wr_hi.shape[0]
    fix = lambda i: (0, 0)
    per_tok = pl.BlockSpec((SUBLANES, tm), lambda i: (0, i))
    return pl.pallas_call(
        _route_kernel,
        grid=(t // tm,),
        in_specs=[pl.BlockSpec((tm, d), lambda i: (i, 0)),
                  pl.BlockSpec((n_e, d), fix), pl.BlockSpec((n_e, d), fix),
                  pl.BlockSpec((n_e, 1), fix), pl.BlockSpec((n_e, LANES), fix)],
        out_specs=[per_tok, per_tok, per_tok, pl.BlockSpec((n_e, LANES), fix)],
        out_shape=[jax.ShapeDtypeStruct((SUBLANES, t), I32), jax.ShapeDtypeStruct((SUBLANES, t), F32),
                   jax.ShapeDtypeStruct((SUBLANES, t), I32), jax.ShapeDtypeStruct((n_e, LANES), F32)],
        name="route", compiler_params=_params(("arbitrary",), 32),
    )(h2, wr_hi, wr_lo, b_router.reshape(n_e, 1), counts_in)


def _row_copy(src, dst, sem, src_row, dst_row):
    return pltpu.make_async_copy(src.at[pl.ds(src_row * ROW_PITCH, ROW_CHUNKS), :],
                                 dst.at[pl.ds(dst_row * ROW_CHUNKS, ROW_CHUNKS), :], sem)


def _dispatch_kernel(dest_ref, hp_ref, hs_ref, xs_ref, rows, sem, *, prompt_tiles):
    tm = hp_ref.shape[0]
    h = jnp.where(pl.program_id(0) < prompt_tiles, hp_ref[...], hs_ref[...]).astype(F32)
    for cidx in range(ROW_CHUNKS):
        rows[pl.ds(cidx, tm, stride=ROW_PITCH), :] = h[:, cidx * LANES:(cidx + 1) * LANES]

    def start(t, _):
        for k in range(TOP_K):
            _row_copy(rows, xs_ref, sem, t, dest_ref[k, t]).start()
        return 0

    def wait(t, _):
        for k in range(TOP_K):
            _row_copy(rows, xs_ref, sem, t, dest_ref[k, t]).wait()
        return 0

    lax.fori_loop(0, tm, start, 0)
    lax.fori_loop(0, tm, wait, 0)


def _dispatch(dest, h2_p, h2_s, n_rows, tm):
    tp, d = h2_p.shape
    ts = h2_s.shape[0]
    n_p, n_s = tp // tm, ts // tm
    return pl.pallas_call(
        functools.partial(_dispatch_kernel, prompt_tiles=n_p),
        grid=(n_p + n_s,),
        in_specs=[pl.BlockSpec((SUBLANES, tm), lambda i: (0, i), memory_space=pltpu.SMEM),
                  pl.BlockSpec((tm, d), lambda i: (jnp.minimum(i, n_p - 1), 0)),
                  pl.BlockSpec((tm, d), lambda i: (jnp.maximum(i - n_p, 0), 0))],
        out_specs=pl.BlockSpec(memory_space=pl.ANY),
        out_shape=jax.ShapeDtypeStruct((n_rows * ROW_CHUNKS, LANES), F32),
        scratch_shapes=[pltpu.VMEM((tm * ROW_PITCH, LANES), F32), pltpu.SemaphoreType.DMA(())],
        name="dispatch", compiler_params=_params(("arbitrary",), 32),
    )(dest, h2_p, h2_s)


def _expert_kernel(se_ref, sb_ref, lo_ref, hi_ref, xs_ref, wg_ref, wu_ref, wd_ref, y_ref, wg_b, wu_b, wd_b, acc):
    s = pl.program_id(0)
    prev = jnp.maximum(s - 1, 0)
    first = s == 0

    @pl.when(jnp.logical_or(first, se_ref[s] != se_ref[prev]))
    def _():
        wg_b[...] = wg_ref[0].astype(BF16)
        wu_b[...] = wu_ref[0].astype(BF16)
        wd_b[...] = wd_ref[0].astype(BF16)

    rb = xs_ref.shape[0] // ROW_CHUNKS
    x = jnp.concatenate([xs_ref[pl.ds(cidx, rb, stride=ROW_CHUNKS), :] for cidx in range(ROW_CHUNKS)],
                        axis=1).astype(BF16)
    g = jnp.dot(x, wg_b[...], preferred_element_type=F32)
    u = jnp.dot(x, wu_b[...], preferred_element_type=F32)
    a = ((g * _sigmoid(g)) * u).astype(BF16)
    y = jnp.dot(a, wd_b[...], preferred_element_type=F32)
    row = sb_ref[s] * rb + lax.broadcasted_iota(I32, (rb, 1), 0)
    y = jnp.where(jnp.logical_and(row >= lo_ref[s], row < hi_ref[s]), y, 0.0)
    new_block = jnp.logical_or(first, sb_ref[s] != sb_ref[prev])

    @pl.when(new_block)
    def _():
        acc[...] = y

    @pl.when(jnp.logical_not(new_block))
    def _():
        acc[...] = acc[...] + y

    for cidx in range(ROW_CHUNKS):
        y_ref[pl.ds(cidx, rb, stride=ROW_CHUNKS), :] = acc[:, cidx * LANES:(cidx + 1) * LANES]


def _experts(step_expert, step_block, row_lo, row_hi, xs, w_gate, w_up, w_down):
    n_steps = step_expert.shape[0]
    _, d, de = w_gate.shape
    rows = EXPERT_ROWS * ROW_CHUNKS
    by_expert = lambda s, se, sb, lo, hi: (se[s], 0, 0)
    by_block = lambda s, se, sb, lo, hi: (sb[s], 0)
    grid_spec = pltpu.PrefetchScalarGridSpec(
        num_scalar_prefetch=4,
        grid=(n_steps,),
        in_specs=[pl.BlockSpec((rows, LANES), by_block),
                  pl.BlockSpec((1, d, de), by_expert),
                  pl.BlockSpec((1, d, de), by_expert),
                  pl.BlockSpec((1, de, d), by_expert)],
        out_specs=pl.BlockSpec((rows, LANES), by_block),
        scratch_shapes=[pltpu.VMEM((d, de), BF16), pltpu.VMEM((d, de), BF16), pltpu.VMEM((de, d), BF16),
                        pltpu.VMEM((EXPERT_ROWS, d), F32)],
    )
    return pl.pallas_call(
        _expert_kernel,
        grid_spec=grid_spec,
        out_shape=jax.ShapeDtypeStruct(xs.shape, F32),
        name="experts", compiler_params=_params(("arbitrary",), 56),
    )(step_expert, step_block, row_lo, row_hi, xs, w_gate, w_up, w_down)


def _combine_kernel(dest_ref, gates_ref, h_ref, x_ref, wg_ref, wu_ref, wd_ref, gt_ref, gfin_ref, sh_ref, sc_ref,
                    y_hbm, o_ref, buf, sem):
    tm = h_ref.shape[0]

    def copy(t, k):
        return pltpu.make_async_copy(
            y_hbm.at[pl.ds(dest_ref[k, t] * ROW_CHUNKS, ROW_CHUNKS), :],
            buf.at[k, pl.ds(t * ROW_PITCH, ROW_CHUNKS), :], sem)

    def start(t, _):
        for k in range(TOP_K):
            copy(t, k).start()
        return 0

    def wait(t, _):
        for k in range(TOP_K):
            copy(t, k).wait()
        return 0

    lax.fori_loop(0, tm, start, 0)

    h = h_ref[...]
    g = jnp.dot(h, wg_ref[...], preferred_element_type=F32)
    u = jnp.dot(h, wu_ref[...], preferred_element_type=F32)
    shared = jnp.dot(((g * _sigmoid(g)) * u).astype(BF16), wd_ref[...], preferred_element_type=F32)

    lax.fori_loop(0, tm, wait, 0)

    gates = gates_ref[...]
    routed = jnp.zeros(shared.shape, F32)
    for k in range(TOP_K):
        yk = jnp.concatenate([buf[k, pl.ds(cidx, tm, stride=ROW_PITCH), :] for cidx in range(ROW_CHUNKS)], axis=1)
        routed = routed + yk * gates[:, k:k + 1]
    xo = x_ref[...] + _row(gt_ref) * (routed + shared)
    o_ref[...] = (_rms(xo) * gfin_ref[...]) * (1.0 + _row(sc_ref)) + _row(sh_ref)


def _combine(dest, gates_tok, h2, x, w_sg, w_su, w_sd, mod, mod_out, per_seq, tokens_per_seq, g_final, y_rows, tm):
    t, d = x.shape
    ds = w_sg.shape[1]
    tok = lambda i: (i, 0)
    fix = lambda i: (0, 0)
    return pl.pallas_call(
        _combine_kernel,
        grid=(t // tm,),
        in_specs=[pl.BlockSpec((SUBLANES, tm), lambda i: (0, i), memory_space=pltpu.SMEM),
                  pl.BlockSpec((tm, SUBLANES), tok),
                  pl.BlockSpec((tm, d), tok), pl.BlockSpec((tm, d), tok),
                  pl.BlockSpec((d, ds), fix), pl.BlockSpec((d, ds), fix), pl.BlockSpec((ds, d), fix),
                  _mod_spec(per_seq, tm, d, 5, tokens_per_seq),
                  pl.BlockSpec((1, d), fix),
                  _mod_spec(per_seq, tm, d, 0, tokens_per_seq),
                  _mod_spec(per_seq, tm, d, 1, tokens_per_seq),
                  pl.BlockSpec(memory_space=pl.ANY)],
        out_specs=pl.BlockSpec((tm, d), tok),
        out_shape=jax.ShapeDtypeStruct((t, d), F32),
        scratch_shapes=[pltpu.VMEM((TOP_K, tm * ROW_PITCH, LANES), F32), pltpu.SemaphoreType.DMA(())],
        name="combine", compiler_params=_params(("arbitrary",), 48),
    )(dest, gates_tok, h2, x, w_sg, w_su, w_sd, mod, g_final.reshape(1, d), mod_out, mod_out, y_rows)


def kernel(x_prompt, x_sample, cache_k, cache_v, state_conv, page_table, c_prompt, c_sample, w_ada, b_ada, g_mix, w_in, b_sb, conv_w, conv_b, conv_ln_g, conv_ln_b, g_sb_out, w_o, g_ffn, w_router, b_router, w_exp_gate, w_exp_up, w_exp_down, w_sh_gate, w_sh_up, w_sh_down, w_ada_out, b_ada_out, g_final):
    bsz, seq, d = x_prompt.shape
    dbsz, dseq, _ = x_sample.shape
    depth, n_pool, page, heads, head_dim = cache_k.shape
    assert depth == 1 and d == ROW_CHUNKS * LANES
    dsb = heads * head_dim
    dcv = conv_w.shape[-1]
    width = conv_w.shape[1]
    n_e = w_router.shape[-1]
    tp, ts = bsz * seq, dbsz * dseq
    assert dsb == dcv and w_in.shape[-1] == 5 * dsb and width - 1 <= CONV_HALO

    c_all = jnp.concatenate([c_prompt, c_sample], axis=0)
    pad = (-c_all.shape[0]) % SUBLANES
    c_all = jnp.pad(c_all, ((0, pad), (0, 0)))
    mod = _ada(c_all, w_ada[0], b_ada[0])
    mod_out = _ada(c_all, w_ada_out, b_ada_out)
    mod_p = mod[:bsz].reshape(bsz, 1, 6 * d)
    modo_p = mod_out[:bsz].reshape(bsz, 1, 2 * d)
    mod_s = jnp.repeat(mod[bsz:bsz + dbsz], dseq, axis=0)
    modo_s = jnp.repeat(mod_out[bsz:bsz + dbsz], dseq, axis=0)

    xp = x_prompt.reshape(tp, d)
    xs = x_sample.reshape(ts, d)
    w_in_b = w_in[0].astype(BF16)
    q_scale = head_dim ** -0.5

    q_p, k_p, v_p, u_p = _inproj(xp, mod_p, True, seq, g_mix[0], w_in_b, 512, q_scale)
    o_sb_p = _prompt_attention(q_p, k_p, v_p, b_sb[0], bsz, seq, head_dim)
    u_p3 = u_p.reshape(bsz, seq, dcv)
    o_cv_p = _conv(u_p3, u_p3, conv_w[0], conv_b[0], conv_ln_g[0], conv_ln_b[0], 256, True).reshape(tp, dcv)
    w_o_b = w_o[0].astype(BF16)
    x1_p, h2_p = _outproj(o_sb_p, o_cv_p, xp, g_sb_out[0], w_o_b[:dsb], w_o_b[dsb:], mod_p, True, seq, g_ffn[0], 256)

    q_s, k_s, v_s, u_s = _inproj(xs, mod_s, False, 1, g_mix[0], w_in_b, ts, q_scale)
    rows = heads * dseq
    eye = jnp.eye(heads, dtype=q_s.dtype)
    qbd = jnp.einsum('bqhd,hg->bhqgd', q_s.reshape(dbsz, dseq, heads, head_dim), eye).reshape(dbsz, rows, dsb)
    bias_col = jnp.repeat(b_sb[0], dseq).reshape(rows, 1)
    new_page = lambda z: jnp.pad(z.reshape(dbsz, dseq, dsb).transpose(0, 2, 1), ((0, 0), (0, 0), (0, page - dseq)))
    pages = lambda cache: cache[0].transpose(0, 2, 3, 1).reshape(n_pool, dsb, page)
    folded = _sample_attention(qbd, bias_col, new_page(k_s), new_page(v_s), pages(cache_k), pages(cache_v),
                               page_table, dseq, head_dim)
    folded = folded.reshape(dbsz, heads, dseq, 2, head_dim)
    o_sb_s = (folded[:, :, :, 0] + folded[:, :, :, 1]).transpose(0, 2, 1, 3).reshape(ts, dsb)
    u_s3 = u_s.reshape(dbsz, dseq, dcv)
    u_ext_s = jnp.concatenate([state_conv[0], u_s3], axis=1)
    halo_s = jnp.pad(state_conv[0], ((0, 0), (CONV_HALO - (width - 1), 0), (0, 0)))
    rows_s = 2 * SUBLANES
    u_s16 = jnp.pad(u_s3, ((0, 0), (0, rows_s - dseq), (0, 0)))
    o_cv_s = _conv(halo_s, u_s16, conv_w[0], conv_b[0], conv_ln_g[0], conv_ln_b[0], rows_s, False)
    o_cv_s = o_cv_s[:, :dseq].reshape(ts, dcv)
    x1_s, h2_s = _outproj(o_sb_s, o_cv_s, xs, g_sb_out[0], w_o_b[:dsb], w_o_b[dsb:], mod_s, False, 1, g_ffn[0], ts)

    wr_t = w_router[0].T
    wr_hi = wr_t.astype(BF16)
    wr_lo = (wr_t - wr_hi.astype(F32)).astype(BF16)
    zero_counts = jnp.zeros((n_e, LANES), F32)
    idx_p, gate_p, pos_p, counts = _route(h2_p, wr_hi, wr_lo, b_router[0], zero_counts, 512)
    idx_s, gate_s, pos_s, counts = _route(h2_s, wr_hi, wr_lo, b_router[0], counts, ts)

    n_assign = (tp + ts) * TOP_K
    assert n_assign % EXPERT_ROWS == 0
    n_blocks = n_assign // EXPERT_ROWS
    n_steps = n_blocks + n_e - 1
    cnt = counts[:, 0].astype(I32)
    g_end = jnp.cumsum(cnt)
    g_start = g_end - cnt
    first_blk = g_start // EXPERT_ROWS
    visits = jnp.where(cnt > 0, (g_end - 1) // EXPERT_ROWS - first_blk + 1, 0)
    s_end = jnp.cumsum(visits)
    s_start = s_end - visits
    step = jnp.arange(n_steps, dtype=I32)
    step_expert = jnp.minimum(jnp.sum((s_end[None, :] <= step[:, None]).astype(I32), axis=1), n_e - 1)
    live = step < s_end[n_e - 1]
    step_block = jnp.where(live, first_blk[step_expert] + step - s_start[step_expert], n_blocks - 1).astype(I32)
    row_lo = jnp.where(live, g_start[step_expert], 0).astype(I32)
    row_hi = jnp.where(live, g_end[step_expert], 0).astype(I32)
    experts = jnp.arange(n_e, dtype=I32)

    def slots(idx, pos):
        return pos + jnp.sum(jnp.where(idx[:, :, None] == experts, g_start, 0), axis=-1).astype(I32)

    dest_p, dest_s = slots(idx_p, pos_p), slots(idx_s, pos_s)
    xs_rows = _dispatch(jnp.concatenate([dest_p, dest_s], axis=1), h2_p, h2_s, n_assign, min(128, ts))
    y_rows = _experts(step_expert.astype(I32), step_block, row_lo, row_hi, xs_rows, w_exp_gate[0], w_exp_up[0],
                      w_exp_down[0])

    w_sg, w_su, w_sd = w_sh_gate[0].astype(BF16), w_sh_up[0].astype(BF16), w_sh_down[0].astype(BF16)
    y_p = _combine(dest_p, gate_p.T, h2_p, x1_p, w_sg, w_su, w_sd, mod_p, modo_p, True, seq, g_final, y_rows, 128)
    y_s = _combine(dest_s, gate_s.T, h2_s, x1_s, w_sg, w_su, w_sd, mod_s, modo_s, False, 1, g_final, y_rows, ts)

    kv = lambda z, b, t: z.reshape(1, b, t, heads, head_dim)
    return (y_p.reshape(bsz, seq, d), y_s.reshape(dbsz, dseq, d),
            kv(k_p, bsz, seq), kv(v_p, bsz, seq), u_p3[:, seq - (width - 1):][None],
            kv(k_s, dbsz, dseq), kv(v_s, dbsz, dseq), u_ext_s[:, dseq:][None])
```

```python
import functools

import jax
import jax.numpy as jnp
from jax import lax
from jax.experimental import pallas as pl
from jax.experimental.pallas import tpu as pltpu

F32 = jnp.float32
BF16 = jnp.bfloat16
I32 = jnp.int32

TOP_K = 6
N_EXPERT_GROUPS = 8
TOPK_GROUPS = 4
ROUTED_SCALE = 2.5
RMS_EPS = 1e-6
LN_EPS = 1e-5

LANES = 128
SUBLANES = 8
VMEM_BYTES_V7X = 64 * 1024 * 1024

ATTN_TILE = 256
PAGES_PER_STEP = 8
EXPERT_ROWS = 256
CONV_HALO = 32
ROW_CHUNKS = 16
ROW_PITCH = 20


def _params(sem, vmem_mib):
    return pltpu.CompilerParams(dimension_semantics=sem, vmem_limit_bytes=vmem_mib * 1024 * 1024)


def _row(ref):
    return ref[0] if len(ref.shape) == 3 else ref[...]


def _sigmoid(x):
    return 1.0 / (1.0 + jnp.exp(-x))


def _softplus(z):
    return jnp.maximum(z, 0.0) + jnp.log(1.0 + jnp.exp(-jnp.abs(z)))


def _rms(x):
    return x * lax.rsqrt(jnp.mean(x * x, axis=-1, keepdims=True) + RMS_EPS)


def _mod_spec(per_seq, tm, d, chunk, tokens_per_seq):
    if per_seq:
        return pl.BlockSpec((1, 1, d), lambda i, *_: ((i * tm) // tokens_per_seq, 0, chunk))
    return pl.BlockSpec((tm, d), lambda i, *_: (i, chunk))


def _ada_kernel(c_ref, w_ref, b_ref, o_ref):
    c = c_ref[...]
    s = (c * _sigmoid(c)).astype(BF16)
    o_ref[...] = jnp.dot(s, w_ref[...].astype(BF16), preferred_element_type=F32) + b_ref[...]


def _ada(c, w, b, tn=512):
    m, d = c.shape
    n = w.shape[1]
    return pl.pallas_call(
        _ada_kernel,
        grid=(n // tn,),
        in_specs=[pl.BlockSpec((m, d), lambda j: (0, 0)),
                  pl.BlockSpec((d, tn), lambda j: (0, j)),
                  pl.BlockSpec((1, tn), lambda j: (0, j))],
        out_specs=pl.BlockSpec((m, tn), lambda j: (0, j)),
        out_shape=jax.ShapeDtypeStruct((m, n), F32),
        name="ada", compiler_params=_params(("arbitrary",), 32),
    )(c, w, b.reshape(1, n))


def _inproj_kernel(x_ref, sh_ref, sc_ref, g_ref, w_ref, q_ref, k_ref, v_ref, u_ref, h_scr, a_scr, *, q_scale,
                   feature_major_kv):
    j = pl.program_id(1)

    def put_kv(ref, p):
        if feature_major_kv:
            ref[0] = p.T
        else:
            ref[...] = p

    @pl.when(j == 0)
    def _():
        h = (_rms(x_ref[...]) * g_ref[...]) * (1.0 + _row(sc_ref)) + _row(sh_ref)
        h_scr[...] = h.astype(BF16)

    p = jnp.dot(h_scr[...], w_ref[...], preferred_element_type=F32)

    @pl.when(j == 0)
    def _():
        q_ref[...] = (p * q_scale).astype(BF16)

    @pl.when(j == 1)
    def _():
        put_kv(k_ref, p)

    @pl.when(j == 2)
    def _():
        put_kv(v_ref, p)

    @pl.when(j == 3)
    def _():
        a_scr[...] = p

    @pl.when(j == 4)
    def _():
        u_ref[...] = a_scr[...] * _sigmoid(p)


def _inproj(x, mod, per_seq, tokens_per_seq, g_mix, w_in_bf16, tm, q_scale):
    t, d = x.shape
    gw = w_in_bf16.shape[1] // 5
    tok = lambda i, j: (i, 0)
    out = jax.ShapeDtypeStruct((t, gw), F32)
    row_major = pl.BlockSpec((tm, gw), tok)
    if per_seq:
        per = tokens_per_seq // tm
        kv_spec = pl.BlockSpec((1, gw, tm), lambda i, j: (i // per, 0, i % per))
        kv_shape = jax.ShapeDtypeStruct((t // tokens_per_seq, gw, tokens_per_seq), F32)
    else:
        kv_spec, kv_shape = row_major, out
    return pl.pallas_call(
        functools.partial(_inproj_kernel, q_scale=q_scale, feature_major_kv=per_seq),
        grid=(t // tm, 5),
        in_specs=[pl.BlockSpec((tm, d), tok),
                  _mod_spec(per_seq, tm, d, 0, tokens_per_seq),
                  _mod_spec(per_seq, tm, d, 1, tokens_per_seq),
                  pl.BlockSpec((1, d), lambda i, j: (0, 0)),
                  pl.BlockSpec((d, gw), lambda i, j: (0, j))],
        out_specs=[row_major, kv_spec, kv_spec, row_major],
        out_shape=[jax.ShapeDtypeStruct((t, gw), BF16), kv_shape, kv_shape, out],
        scratch_shapes=[pltpu.VMEM((tm, d), BF16), pltpu.VMEM((tm, gw), F32)],
        name="inproj", compiler_params=_params(("arbitrary", "arbitrary"), 48),
    )(x, mod, mod, g_mix.reshape(1, d), w_in_bf16)


def _attn_kernel(bias_ref, q_ref, k_ref, v_ref, o_ref, kb, qm, vm, sp_scr, d_scr, *, head_dim, tile):
    hp = pl.program_id(1)
    seq = q_ref.shape[0]
    first = lax.broadcasted_iota(I32, (1, LANES), 1) < head_dim
    first_rows = lax.broadcasted_iota(I32, (LANES, 1), 0) < head_dim
    q = q_ref[...]
    v = v_ref[0].astype(BF16)
    kb[...] = k_ref[0].astype(BF16)
    qm[0] = jnp.where(first, q, jnp.zeros_like(q))
    qm[1] = jnp.where(first, jnp.zeros_like(q), q)
    vm[0] = jnp.where(first_rows, v, jnp.zeros_like(v))
    vm[1] = jnp.where(first_rows, jnp.zeros_like(v), v)

    r = lax.broadcasted_iota(I32, (tile, tile), 0)
    c = lax.broadcasted_iota(I32, (tile, tile), 1)
    neg_later = jnp.concatenate([jnp.where(r > c, -1.0, 0.0), jnp.full((tile, LANES), -1.0, F32)],
                                axis=1).astype(BF16)
    causal = c < r

    nt = (((1,), (1,)), ((), ()))
    heads = (0, 1)
    bias = [bias_ref[2 * hp + h] for h in heads]

    def scores(h, q0, k0):
        return jnp.dot(qm[h, pl.ds(q0, tile), :], kb[:, pl.ds(k0, tile)], preferred_element_type=F32) + bias[h]

    def keep_scores(h, slot, z, masked):
        sp = _softplus(z)
        d = z - sp
        if masked:
            sp = jnp.where(causal, sp, 0.0)
            d = jnp.where(causal, d, -jnp.inf)
        sp_scr[h, slot] = sp.astype(BF16)
        d_scr[h, slot] = d

    def later_sum(h, slot):
        return jnp.dot(sp_scr[h, slot], neg_later, preferred_element_type=F32)

    def weights(h, slot, k0, later, acc, carry):
        wide = jnp.concatenate([carry] * (tile // LANES), axis=1)
        w = jnp.exp(d_scr[h, slot] + (later[:, :tile] + wide))
        acc = acc + lax.dot_general(w.astype(BF16), vm[h, :, pl.ds(k0, tile)], nt, preferred_element_type=F32)
        return acc, carry + later[:, tile:]

    def q_block(i, _):
        q0 = pl.multiple_of(i * tile, tile)
        for h in heads:
            keep_scores(h, 0, scores(h, q0, q0), True)

        def k_block(jj, st):
            slot = jj % 2
            k_prev = pl.multiple_of((i - jj) * tile, tile)
            k_next = pl.multiple_of((i - jj - 1) * tile, tile)
            later = [later_sum(h, slot) for h in heads]
            z = [scores(h, q0, k_next) for h in heads]
            st = [weights(h, slot, k_prev, later[h], *st[h]) for h in heads]
            for h in heads:
                keep_scores(h, 1 - slot, z[h], False)
            return tuple(st)

        init = tuple((jnp.zeros((tile, LANES), F32), jnp.zeros((tile, LANES), F32)) for _ in heads)
        st = lax.fori_loop(0, i, k_block, init)
        last = i % 2
        later = [later_sum(h, last) for h in heads]
        accs = [weights(h, last, 0, later[h], *st[h])[0] for h in heads]
        o_ref[pl.ds(q0, tile), :] = accs[0] + accs[1]
        return 0

    lax.fori_loop(0, seq // tile, q_block, 0)


def _prompt_attention(q, k, v, b_sb, bsz, seq, head_dim):
    t, dsb = q.shape
    blk = pl.BlockSpec((seq, LANES), lambda b, hp: (b, hp))
    kv_blk = pl.BlockSpec((1, LANES, seq), lambda b, hp: (b, hp, 0))
    return pl.pallas_call(
        functools.partial(_attn_kernel, head_dim=head_dim, tile=ATTN_TILE),
        grid=(bsz, dsb // LANES),
        in_specs=[pl.BlockSpec(memory_space=pltpu.SMEM), blk, kv_blk, kv_blk],
        out_specs=blk,
        out_shape=jax.ShapeDtypeStruct((t, dsb), F32),
        scratch_shapes=[pltpu.VMEM((LANES, seq), BF16), pltpu.VMEM((2, seq, LANES), BF16),
                        pltpu.VMEM((2, LANES, seq), BF16),
                        pltpu.VMEM((2, 2, ATTN_TILE, ATTN_TILE), BF16),
                        pltpu.VMEM((2, 2, ATTN_TILE, ATTN_TILE), F32)],
        name="prompt_attn", compiler_params=_params(("arbitrary", "arbitrary"), 48),
    )(b_sb, q, k, v)


def _sattn_kernel(pt_ref, qbd_ref, bias_ref, kn_ref, vn_ref, *rest, n_q, head_dim, page):
    npg = PAGES_PER_STEP
    k_refs, v_refs = rest[:npg], rest[npg:2 * npg]
    o_ref, acc_scr, carry_scr = rest[2 * npg:]
    step = pl.program_id(1)
    rows = qbd_ref.shape[1]
    qbd = qbd_ref[0]
    bias = bias_ref[...]
    r = lax.broadcasted_iota(I32, (page, page), 0)
    c = lax.broadcasted_iota(I32, (page, page), 1)
    neg_later = jnp.where(r > c, -1.0, 0.0).astype(BF16)

    def block(kp, vp, mask, acc, carry):
        z = jnp.dot(qbd, kp.astype(BF16), preferred_element_type=F32) + bias
        sp = _softplus(z)
        if mask is not None:
            sp = jnp.where(mask, sp, 0.0)
        later = jnp.dot(sp.astype(BF16), neg_later, preferred_element_type=F32) + carry
        w = jnp.exp(z - sp + later)
        if mask is not None:
            w = jnp.where(mask, w, 0.0)
        acc = acc + lax.dot_general(w.astype(BF16), vp.astype(BF16), (((1,), (1,)), ((), ())),
                                    preferred_element_type=F32)
        return acc, carry - jnp.sum(sp, axis=1, keepdims=True)

    @pl.when(step == 0)
    def _():
        qi = lax.broadcasted_iota(I32, (rows, page), 0) % n_q
        kj = lax.broadcasted_iota(I32, (rows, page), 1)
        acc, carry = block(kn_ref[0], vn_ref[0], kj < qi, jnp.zeros(acc_scr.shape, F32),
                           jnp.zeros((rows, 1), F32))
        acc_scr[...] = acc
        carry_scr[...] = carry

    order = list(reversed(range(npg)))
    zs = [jnp.dot(qbd, k_refs[j][0].astype(BF16), preferred_element_type=F32) + bias for j in order]
    sps = [_softplus(z) for z in zs]
    later_all = jnp.dot(jnp.concatenate([sp.astype(BF16) for sp in sps], axis=0), neg_later,
                        preferred_element_type=F32)
    carry = carry_scr[...]
    ws = []
    for n in range(npg):
        later = later_all[n * rows:(n + 1) * rows, :] + carry
        ws.append(jnp.exp(zs[n] - sps[n] + later).astype(BF16))
        carry = carry - jnp.sum(sps[n], axis=1, keepdims=True)
    acc = acc_scr[...]
    for n, j in enumerate(order):
        acc = acc + lax.dot_general(ws[n], v_refs[j][0].astype(BF16), (((1,), (1,)), ((), ())),
                                    preferred_element_type=F32)
    acc_scr[...] = acc
    carry_scr[...] = carry

    @pl.when(step == pl.num_programs(1) - 1)
    def _():
        width = acc_scr.shape[1]
        rh = lax.broadcasted_iota(I32, (rows, width), 0) // n_q
        ch = lax.broadcasted_iota(I32, (rows, width), 1) // head_dim
        own = jnp.where(rh == ch, acc, 0.0)
        folded = own[:, 0:LANES]
        for b in range(1, width // LANES):
            folded = folded + own[:, b * LANES:(b + 1) * LANES]
        o_ref[0] = folded


def _sample_attention(qbd, bias_col, k_new, v_new, cache_k, cache_v, page_table, n_q, head_dim):
    dbsz, rows, dsb = qbd.shape
    n_pages = page_table.shape[1]
    page = cache_k.shape[2]
    steps = n_pages // PAGES_PER_STEP

    def page_spec(j):
        return pl.BlockSpec((1, dsb, page),
                            lambda b, s, pt: (pt[b, (steps - 1 - s) * PAGES_PER_STEP + j], 0, 0))

    per_b = lambda b, s, pt: (b, 0, 0)
    grid_spec = pltpu.PrefetchScalarGridSpec(
        num_scalar_prefetch=1,
        grid=(dbsz, steps),
        in_specs=[pl.BlockSpec((1, rows, dsb), per_b),
                  pl.BlockSpec((rows, 1), lambda b, s, pt: (0, 0)),
                  pl.BlockSpec((1, dsb, page), per_b),
                  pl.BlockSpec((1, dsb, page), per_b)]
                 + [page_spec(j) for j in range(PAGES_PER_STEP)] * 2,
        out_specs=pl.BlockSpec((1, rows, LANES), per_b),
        scratch_shapes=[pltpu.VMEM((rows, dsb), F32), pltpu.VMEM((rows, 1), F32)],
    )
    return pl.pallas_call(
        functools.partial(_sattn_kernel, n_q=n_q, head_dim=head_dim, page=page),
        grid_spec=grid_spec,
        out_shape=jax.ShapeDtypeStruct((dbsz, rows, LANES), F32),
        name="sample_attn", compiler_params=_params(("arbitrary", "arbitrary"), 48),
    )(page_table, qbd, bias_col, k_new, v_new, *([cache_k] * PAGES_PER_STEP), *([cache_v] * PAGES_PER_STEP))


def _conv_kernel(halo_ref, u_ref, cw_ref, cb_ref, lg_ref, lb_ref, o_ref, ext, y_scr, *, width, zero_first, sub):
    tt, ch = u_ref.shape[1], u_ref.shape[2]
    halo = halo_ref[0]
    if zero_first:
        halo = jnp.where(pl.program_id(1) == 0, 0.0, halo)
    ext[0:CONV_HALO, :] = halo
    ext[CONV_HALO:CONV_HALO + tt, :] = u_ref[0]
    ext[CONV_HALO + tt:CONV_HALO + tt + SUBLANES, :] = jnp.zeros((SUBLANES, ch), F32)
    lead = CONV_HALO - (width - 1)

    def lane_block(cb, _):
        c0 = pl.multiple_of(cb * LANES, LANES)
        for ts in range(tt // sub):
            partial = [None] * SUBLANES
            for a in range((lead + width - 1) // SUBLANES + 1):
                rows = ext[pl.ds(ts * sub + SUBLANES * a, sub + SUBLANES), pl.ds(c0, LANES)]
                for r in range(SUBLANES):
                    w = SUBLANES * a + r - lead
                    if 0 <= w < width:
                        term = rows * cw_ref[pl.ds(w, 1), pl.ds(c0, LANES)]
                        partial[r] = term if partial[r] is None else partial[r] + term
            acc = partial[0][0:sub, :]
            for r in range(1, SUBLANES):
                acc = acc + partial[r][r:r + sub, :]
            y_scr[pl.ds(ts * sub, sub), pl.ds(c0, LANES)] = acc
        return 0

    lax.fori_loop(0, ch // LANES, lane_block, 0)
    y = y_scr[...] + cb_ref[...]
    mu = jnp.mean(y, axis=-1, keepdims=True)
    yc = y - mu
    var = jnp.mean(yc * yc, axis=-1, keepdims=True)
    yn = yc * lax.rsqrt(var + LN_EPS) * lg_ref[...] + lb_ref[...]
    o_ref[0] = (yn * _sigmoid(yn)).astype(BF16)


def _conv(halo_src, u3, conv_w, conv_b, ln_g, ln_b, tt, halo_from_u):
    b, t, ch = u3.shape
    width = conv_w.shape[0]
    if halo_from_u:
        halo_spec = pl.BlockSpec((1, CONV_HALO, ch),
                                 lambda bi, i: (bi, jnp.maximum(i * (tt // CONV_HALO) - 1, 0), 0))
    else:
        halo_spec = pl.BlockSpec((1, CONV_HALO, ch), lambda bi, i: (bi, 0, 0))
    vec = pl.BlockSpec((1, ch), lambda bi, i: (0, 0))
    return pl.pallas_call(
        functools.partial(_conv_kernel, width=width, zero_first=halo_from_u, sub=min(tt, 32)),
        grid=(b, t // tt),
        in_specs=[halo_spec,
                  pl.BlockSpec((1, tt, ch), lambda bi, i: (bi, i, 0)),
                  pl.BlockSpec((width, ch), lambda bi, i: (0, 0)),
                  vec, vec, vec],
        out_specs=pl.BlockSpec((1, tt, ch), lambda bi, i: (bi, i, 0)),
        out_shape=jax.ShapeDtypeStruct((b, t, ch), BF16),
        scratch_shapes=[pltpu.VMEM((CONV_HALO + tt + SUBLANES, ch), F32), pltpu.VMEM((tt, ch), F32)],
        name="conv", compiler_params=_params(("arbitrary", "arbitrary"), 32),
    )(halo_src, u3, conv_w, conv_b.reshape(1, ch), ln_g.reshape(1, ch), ln_b.reshape(1, ch))


def _outproj_kernel(osb_ref, ocv_ref, x_ref, gsb_ref, wsb_ref, wcv_ref, gate_ref, gffn_ref, shf_ref, scf_ref,
                    xo_ref, h2_ref):
    osb = (_rms(osb_ref[...]) * gsb_ref[...]).astype(BF16)
    y = jnp.dot(osb, wsb_ref[...], preferred_element_type=F32)
    y = y + jnp.dot(ocv_ref[...], wcv_ref[...], preferred_element_type=F32)
    xn = x_ref[...] + _row(gate_ref) * y
    xo_ref[...] = xn
    h2 = (_rms(xn) * gffn_ref[...]) * (1.0 + _row(scf_ref)) + _row(shf_ref)
    h2_ref[...] = h2.astype(BF16)


def _outproj(o_sb, o_cv, x, g_sb_out, w_sb, w_cv, mod, per_seq, tokens_per_seq, g_ffn, tm):
    t, d = x.shape
    dsb, dcv = o_sb.shape[1], o_cv.shape[1]
    tok = lambda i: (i, 0)
    fix = lambda i: (0, 0)
    return pl.pallas_call(
        _outproj_kernel,
        grid=(t // tm,),
        in_specs=[pl.BlockSpec((tm, dsb), tok), pl.BlockSpec((tm, dcv), tok), pl.BlockSpec((tm, d), tok),
                  pl.BlockSpec((1, dsb), fix), pl.BlockSpec((dsb, d), fix), pl.BlockSpec((dcv, d), fix),
                  _mod_spec(per_seq, tm, d, 2, tokens_per_seq),
                  pl.BlockSpec((1, d), fix),
                  _mod_spec(per_seq, tm, d, 3, tokens_per_seq),
                  _mod_spec(per_seq, tm, d, 4, tokens_per_seq)],
        out_specs=[pl.BlockSpec((tm, d), tok), pl.BlockSpec((tm, d), tok)],
        out_shape=[jax.ShapeDtypeStruct((t, d), F32), jax.ShapeDtypeStruct((t, d), BF16)],
        name="outproj", compiler_params=_params(("arbitrary",), 48),
    )(o_sb, o_cv, x, g_sb_out.reshape(1, dsb), w_sb, w_cv, mod, g_ffn.reshape(1, d), mod, mod)


def _route_kernel(h_ref, whi_ref, wlo_ref, br_ref, cin_ref, idx_ref, gate_ref, pos_ref, cnt_ref):
    n_e = whi_ref.shape[0]
    tm = h_ref.shape[0]
    gsz = n_e // N_EXPERT_GROUPS
    neg = -jnp.inf

    @pl.when(pl.program_id(0) == 0)
    def _():
        cnt_ref[...] = cin_ref[...]

    h = h_ref[...]
    nt = (((1,), (1,)), ((), ()))
    logits = (lax.dot_general(whi_ref[...], h, nt, preferred_element_type=F32)
              + lax.dot_general(wlo_ref[...], h, nt, preferred_element_type=F32))
    scores = _sigmoid(logits)
    biased = scores + br_ref[...]

    sub = lax.broadcasted_iota(I32, (gsz, tm), 0).astype(F32)

    def top1(x, iota, size):
        m = jnp.max(x, axis=0, keepdims=True)
        first = jnp.min(jnp.where(x == m, iota, float(size)), axis=0, keepdims=True)
        return m, iota == first

    gscore = []
    for g in range(N_EXPERT_GROUPS):
        xg = biased[g * gsz:(g + 1) * gsz, :]
        m1, hit = top1(xg, sub, gsz)
        m2 = jnp.max(jnp.where(hit, neg, xg), axis=0, keepdims=True)
        gscore.append(m1 + m2)

    chosen = [jnp.zeros((1, tm), jnp.bool_) for _ in range(N_EXPERT_GROUPS)]
    for _ in range(TOPK_GROUPS):
        m = functools.reduce(jnp.maximum, gscore)
        found = jnp.zeros((1, tm), jnp.bool_)
        for g in range(N_EXPERT_GROUPS):
            take = jnp.logical_and(gscore[g] == m, jnp.logical_not(found))
            found = jnp.logical_or(found, take)
            chosen[g] = jnp.logical_or(chosen[g], take)
            gscore[g] = jnp.where(take, neg, gscore[g])

    masked = jnp.concatenate(
        [jnp.where(chosen[g], biased[g * gsz:(g + 1) * gsz, :], neg) for g in range(N_EXPERT_GROUPS)], axis=0)

    eio = lax.broadcasted_iota(I32, (n_e, tm), 0).astype(F32)
    hits, raw = [], []
    for _ in range(TOP_K):
        _, hit = top1(masked, eio, n_e)
        hits.append(hit)
        raw.append(jnp.sum(jnp.where(hit, scores, 0.0), axis=0, keepdims=True))
        masked = jnp.where(hit, neg, masked)
    denom = functools.reduce(jnp.add, raw)

    sel = functools.reduce(jnp.logical_or, hits)
    sel_f = jnp.where(sel, 1.0, 0.0)
    tr = lax.broadcasted_iota(I32, (tm, tm), 0)
    tc = lax.broadcasted_iota(I32, (tm, tm), 1)
    before = jnp.where(tr < tc, 1.0, 0.0).astype(BF16)
    slot = jnp.dot(sel_f.astype(BF16), before, preferred_element_type=F32) + cnt_ref[:, 0:1]

    idx_ref[...] = jnp.zeros(idx_ref.shape, I32)
    gate_ref[...] = jnp.zeros(gate_ref.shape, F32)
    pos_ref[...] = jnp.zeros(pos_ref.shape, I32)
    for k in range(TOP_K):
        idx_ref[k:k + 1, :] = jnp.sum(jnp.where(hits[k], eio, 0.0), axis=0, keepdims=True).astype(I32)
        gate_ref[k:k + 1, :] = raw[k] / denom * ROUTED_SCALE
        pos_ref[k:k + 1, :] = jnp.sum(jnp.where(hits[k], slot, 0.0), axis=0, keepdims=True).astype(I32)
    cnt_ref[...] = cnt_ref[...] + jnp.sum(sel_f, axis=1, keepdims=True)


def _route(h2, wr_hi, wr_lo, b_router, counts_in, tm):
    t, d = h2.shape
    n_e = wr_hi.shape[0]
    fix = lambda i: (0, 0)
    per_tok = pl.BlockSpec((SUBLANES, tm), lambda i: (0, i))
    return pl.pallas_call(
        _route_kernel,
        grid=(t // tm,),
        in_specs=[pl.BlockSpec((tm, d), lambda i: (i, 0)),
                  pl.BlockSpec((n_e, d), fix), pl.BlockSpec((n_e, d), fix),
                  pl.BlockSpec((n_e, 1), fix), pl.BlockSpec((n_e, LANES), fix)],
        out_specs=[per_tok, per_tok, per_tok, pl.BlockSpec((n_e, LANES), fix)],
        out_shape=[jax.ShapeDtypeStruct((SUBLANES, t), I32), jax.ShapeDtypeStruct((SUBLANES, t), F32),
                   jax.ShapeDtypeStruct((SUBLANES, t), I32), jax.ShapeDtypeStruct((n_e, LANES), F32)],
        name="route", compiler_params=_params(("arbitrary",), 32),
    )(h2, wr_hi, wr_lo, b_router.reshape(n_e, 1), counts_in)


def _row_copy(src, dst, sem, src_row, dst_row):
    return pltpu.make_async_copy(src.at[pl.ds(src_row * ROW_PITCH, ROW_PITCH), :],
                                 dst.at[pl.ds(dst_row * ROW_PITCH, ROW_PITCH), :], sem)


def _dispatch_kernel(dest_ref, hp_ref, hs_ref, xs_ref, rows, sem, *, prompt_tiles):
    tm = hp_ref.shape[0]
    h = jnp.where(pl.program_id(0) < prompt_tiles, hp_ref[...], hs_ref[...]).astype(F32)
    for cidx in range(ROW_CHUNKS):
        rows[pl.ds(cidx, tm, stride=ROW_PITCH), :] = h[:, cidx * LANES:(cidx + 1) * LANES]
    for cidx in range(ROW_CHUNKS, ROW_PITCH):
        rows[pl.ds(cidx, tm, stride=ROW_PITCH), :] = jnp.zeros((tm, LANES), F32)

    def start(t, _):
        for k in range(TOP_K):
            _row_copy(rows, xs_ref, sem, t, dest_ref[k, t]).start()
        return 0

    def wait(t, _):
        for k in range(TOP_K):
            _row_copy(rows, xs_ref, sem, t, dest_ref[k, t]).wait()
        return 0

    lax.fori_loop(0, tm, start, 0)
    lax.fori_loop(0, tm, wait, 0)


def _dispatch(dest, h2_p, h2_s, n_rows, tm):
    tp, d = h2_p.shape
    ts = h2_s.shape[0]
    n_p, n_s = tp // tm, ts // tm
    return pl.pallas_call(
        functools.partial(_dispatch_kernel, prompt_tiles=n_p),
        grid=(n_p + n_s,),
        in_specs=[pl.BlockSpec((SUBLANES, tm), lambda i: (0, i), memory_space=pltpu.SMEM),
                  pl.BlockSpec((tm, d), lambda i: (jnp.minimum(i, n_p - 1), 0)),
                  pl.BlockSpec((tm, d), lambda i: (jnp.maximum(i - n_p, 0), 0))],
        out_specs=pl.BlockSpec(memory_space=pl.ANY),
        out_shape=jax.ShapeDtypeStruct((n_rows * ROW_PITCH, LANES), F32),
        scratch_shapes=[pltpu.VMEM((tm * ROW_PITCH, LANES), F32), pltpu.SemaphoreType.DMA(())],
        name="dispatch", compiler_params=_params(("arbitrary",), 32),
    )(dest, h2_p, h2_s)


def _expert_kernel(se_ref, sb_ref, lo_ref, hi_ref, xs_ref, wg_ref, wu_ref, wd_ref, y_ref, wg_b, wu_b, wd_b, acc):
    s = pl.program_id(0)
    prev = jnp.maximum(s - 1, 0)
    first = s == 0

    @pl.when(jnp.logical_or(first, se_ref[s] != se_ref[prev]))
    def _():
        wg_b[...] = wg_ref[0].astype(BF16)
        wu_b[...] = wu_ref[0].astype(BF16)
        wd_b[...] = wd_ref[0].astype(BF16)

    rb = xs_ref.shape[0] // ROW_PITCH
    x = jnp.concatenate([xs_ref[pl.ds(cidx, rb, stride=ROW_PITCH), :] for cidx in range(ROW_CHUNKS)],
                        axis=1).astype(BF16)
    g = jnp.dot(x, wg_b[...], preferred_element_type=F32)
    u = jnp.dot(x, wu_b[...], preferred_element_type=F32)
    a = ((g * _sigmoid(g)) * u).astype(BF16)
    y = jnp.dot(a, wd_b[...], preferred_element_type=F32)
    row = sb_ref[s] * rb + lax.broadcasted_iota(I32, (rb, 1), 0)
    y = jnp.where(jnp.logical_and(row >= lo_ref[s], row < hi_ref[s]), y, 0.0)
    new_block = jnp.logical_or(first, sb_ref[s] != sb_ref[prev])

    @pl.when(new_block)
    def _():
        acc[...] = y

    @pl.when(jnp.logical_not(new_block))
    def _():
        acc[...] = acc[...] + y

    for cidx in range(ROW_CHUNKS):
        y_ref[pl.ds(cidx, rb, stride=ROW_PITCH), :] = acc[:, cidx * LANES:(cidx + 1) * LANES]
    for cidx in range(ROW_CHUNKS, ROW_PITCH):
        y_ref[pl.ds(cidx, rb, stride=ROW_PITCH), :] = jnp.zeros((rb, LANES), F32)


def _experts(step_expert, step_block, row_lo, row_hi, xs, w_gate, w_up, w_down):
    n_steps = step_expert.shape[0]
    _, d, de = w_gate.shape
    rows = EXPERT_ROWS * ROW_PITCH
    by_expert = lambda s, se, sb, lo, hi: (se[s], 0, 0)
    by_block = lambda s, se, sb, lo, hi: (sb[s], 0)
    grid_spec = pltpu.PrefetchScalarGridSpec(
        num_scalar_prefetch=4,
        grid=(n_steps,),
        in_specs=[pl.BlockSpec((rows, LANES), by_block),
                  pl.BlockSpec((1, d, de), by_expert),
                  pl.BlockSpec((1, d, de), by_expert),
                  pl.BlockSpec((1, de, d), by_expert)],
        out_specs=pl.BlockSpec((rows, LANES), by_block),
        scratch_shapes=[pltpu.VMEM((d, de), BF16), pltpu.VMEM((d, de), BF16), pltpu.VMEM((de, d), BF16),
                        pltpu.VMEM((EXPERT_ROWS, d), F32)],
    )
    return pl.pallas_call(
        _expert_kernel,
        grid_spec=grid_spec,
        out_shape=jax.ShapeDtypeStruct(xs.shape, F32),
        name="experts", compiler_params=_params(("arbitrary",), 56),
    )(step_expert, step_block, row_lo, row_hi, xs, w_gate, w_up, w_down)


def _combine_kernel(dest_ref, gates_ref, h_ref, x_ref, wg_ref, wu_ref, wd_ref, gt_ref, gfin_ref, sh_ref, sc_ref,
                    y_hbm, o_ref, buf, sem):
    tm = h_ref.shape[0]

    def copy(t, k):
        return pltpu.make_async_copy(
            y_hbm.at[pl.ds(dest_ref[k, t] * ROW_PITCH, ROW_CHUNKS), :],
            buf.at[k, pl.ds(t * ROW_PITCH, ROW_CHUNKS), :], sem)

    def start(t, _):
        for k in range(TOP_K):
            copy(t, k).start()
        return 0

    def wait(t, _):
        for k in range(TOP_K):
            copy(t, k).wait()
        return 0

    lax.fori_loop(0, tm, start, 0)

    h = h_ref[...]
    g = jnp.dot(h, wg_ref[...], preferred_element_type=F32)
    u = jnp.dot(h, wu_ref[...], preferred_element_type=F32)
    shared = jnp.dot(((g * _sigmoid(g)) * u).astype(BF16), wd_ref[...], preferred_element_type=F32)

    lax.fori_loop(0, tm, wait, 0)

    gates = gates_ref[...]
    routed = jnp.zeros(shared.shape, F32)
    for k in range(TOP_K):
        yk = jnp.concatenate([buf[k, pl.ds(cidx, tm, stride=ROW_PITCH), :] for cidx in range(ROW_CHUNKS)], axis=1)
        routed = routed + yk * gates[:, k:k + 1]
    xo = x_ref[...] + _row(gt_ref) * (routed + shared)
    o_ref[...] = (_rms(xo) * gfin_ref[...]) * (1.0 + _row(sc_ref)) + _row(sh_ref)


def _combine(dest, gates_tok, h2, x, w_sg, w_su, w_sd, mod, mod_out, per_seq, tokens_per_seq, g_final, y_rows, tm):
    t, d = x.shape
    ds = w_sg.shape[1]
    tok = lambda i: (i, 0)
    fix = lambda i: (0, 0)
    return pl.pallas_call(
        _combine_kernel,
        grid=(t // tm,),
        in_specs=[pl.BlockSpec((SUBLANES, tm), lambda i: (0, i), memory_space=pltpu.SMEM),
                  pl.BlockSpec((tm, SUBLANES), tok),
                  pl.BlockSpec((tm, d), tok), pl.BlockSpec((tm, d), tok),
                  pl.BlockSpec((d, ds), fix), pl.BlockSpec((d, ds), fix), pl.BlockSpec((ds, d), fix),
                  _mod_spec(per_seq, tm, d, 5, tokens_per_seq),
                  pl.BlockSpec((1, d), fix),
                  _mod_spec(per_seq, tm, d, 0, tokens_per_seq),
                  _mod_spec(per_seq, tm, d, 1, tokens_per_seq),
                  pl.BlockSpec(memory_space=pl.ANY)],
        out_specs=pl.BlockSpec((tm, d), tok),
        out_shape=jax.ShapeDtypeStruct((t, d), F32),
        scratch_shapes=[pltpu.VMEM((TOP_K, tm * ROW_PITCH, LANES), F32), pltpu.SemaphoreType.DMA(())],
        name="combine", compiler_params=_params(("arbitrary",), 48),
    )(dest, gates_tok, h2, x, w_sg, w_su, w_sd, mod, g_final.reshape(1, d), mod_out, mod_out, y_rows)


def kernel(x_prompt, x_sample, cache_k, cache_v, state_conv, page_table, c_prompt, c_sample, w_ada, b_ada, g_mix, w_in, b_sb, conv_w, conv_b, conv_ln_g, conv_ln_b, g_sb_out, w_o, g_ffn, w_router, b_router, w_exp_gate, w_exp_up, w_exp_down, w_sh_gate, w_sh_up, w_sh_down, w_ada_out, b_ada_out, g_final):
    bsz, seq, d = x_prompt.shape
    dbsz, dseq, _ = x_sample.shape
    depth, n_pool, page, heads, head_dim = cache_k.shape
    assert depth == 1 and d == ROW_CHUNKS * LANES
    dsb = heads * head_dim
    dcv = conv_w.shape[-1]
    width = conv_w.shape[1]
    n_e = w_router.shape[-1]
    tp, ts = bsz * seq, dbsz * dseq
    assert dsb == dcv and w_in.shape[-1] == 5 * dsb and width - 1 <= CONV_HALO

    c_all = jnp.concatenate([c_prompt, c_sample], axis=0)
    pad = (-c_all.shape[0]) % SUBLANES
    c_all = jnp.pad(c_all, ((0, pad), (0, 0)))
    mod = _ada(c_all, w_ada[0], b_ada[0])
    mod_out = _ada(c_all, w_ada_out, b_ada_out)
    mod_p = mod[:bsz].reshape(bsz, 1, 6 * d)
    modo_p = mod_out[:bsz].reshape(bsz, 1, 2 * d)
    mod_s = jnp.repeat(mod[bsz:bsz + dbsz], dseq, axis=0)
    modo_s = jnp.repeat(mod_out[bsz:bsz + dbsz], dseq, axis=0)

    xp = x_prompt.reshape(tp, d)
    xs = x_sample.reshape(ts, d)
    w_in_b = w_in[0].astype(BF16)
    q_scale = head_dim ** -0.5

    q_p, k_p, v_p, u_p = _inproj(xp, mod_p, True, seq, g_mix[0], w_in_b, 512, q_scale)
    o_sb_p = _prompt_attention(q_p, k_p, v_p, b_sb[0], bsz, seq, head_dim)
    u_p3 = u_p.reshape(bsz, seq, dcv)
    o_cv_p = _conv(u_p3, u_p3, conv_w[0], conv_b[0], conv_ln_g[0], conv_ln_b[0], 256, True).reshape(tp, dcv)
    w_o_b = w_o[0].astype(BF16)
    x1_p, h2_p = _outproj(o_sb_p, o_cv_p, xp, g_sb_out[0], w_o_b[:dsb], w_o_b[dsb:], mod_p, True, seq, g_ffn[0], 256)

    q_s, k_s, v_s, u_s = _inproj(xs, mod_s, False, 1, g_mix[0], w_in_b, ts, q_scale)
    rows = heads * dseq
    eye = jnp.eye(heads, dtype=q_s.dtype)
    qbd = jnp.einsum('bqhd,hg->bhqgd', q_s.reshape(dbsz, dseq, heads, head_dim), eye).reshape(dbsz, rows, dsb)
    bias_col = jnp.repeat(b_sb[0], dseq).reshape(rows, 1)
    new_page = lambda z: jnp.pad(z.reshape(dbsz, dseq, dsb).transpose(0, 2, 1), ((0, 0), (0, 0), (0, page - dseq)))
    pages = lambda cache: cache[0].transpose(0, 2, 3, 1).reshape(n_pool, dsb, page)
    folded = _sample_attention(qbd, bias_col, new_page(k_s), new_page(v_s), pages(cache_k), pages(cache_v),
                               page_table, dseq, head_dim)
    folded = folded.reshape(dbsz, heads, dseq, 2, head_dim)
    o_sb_s = (folded[:, :, :, 0] + folded[:, :, :, 1]).transpose(0, 2, 1, 3).reshape(ts, dsb)
    u_s3 = u_s.reshape(dbsz, dseq, dcv)
    u_ext_s = jnp.concatenate([state_conv[0], u_s3], axis=1)
    halo_s = jnp.pad(state_conv[0], ((0, 0), (CONV_HALO - (width - 1), 0), (0, 0)))
    rows_s = 2 * SUBLANES
    u_s16 = jnp.pad(u_s3, ((0, 0), (0, rows_s - dseq), (0, 0)))
    o_cv_s = _conv(halo_s, u_s16, conv_w[0], conv_b[0], conv_ln_g[0], conv_ln_b[0], rows_s, False)
    o_cv_s = o_cv_s[:, :dseq].reshape(ts, dcv)
    x1_s, h2_s = _outproj(o_sb_s, o_cv_s, xs, g_sb_out[0], w_o_b[:dsb], w_o_b[dsb:], mod_s, False, 1, g_ffn[0], ts)

    wr_t = w_router[0].T
    wr_hi = wr_t.astype(BF16)
    wr_lo = (wr_t - wr_hi.astype(F32)).astype(BF16)
    zero_counts = jnp.zeros((n_e, LANES), F32)
    idx_p, gate_p, pos_p, counts = _route(h2_p, wr_hi, wr_lo, b_router[0], zero_counts, 512)
    idx_s, gate_s, pos_s, counts = _route(h2_s, wr_hi, wr_lo, b_router[0], counts, ts)

    n_assign = (tp + ts) * TOP_K
    assert n_assign % EXPERT_ROWS == 0
    n_blocks = n_assign // EXPERT_ROWS
    n_steps = n_blocks + n_e - 1
    cnt = counts[:, 0].astype(I32)
    g_end = jnp.cumsum(cnt)
    g_start = g_end - cnt
    first_blk = g_start // EXPERT_ROWS
    visits = jnp.where(cnt > 0, (g_end - 1) // EXPERT_ROWS - first_blk + 1, 0)
    s_end = jnp.cumsum(visits)
    s_start = s_end - visits
    step = jnp.arange(n_steps, dtype=I32)
    step_expert = jnp.minimum(jnp.sum((s_end[None, :] <= step[:, None]).astype(I32), axis=1), n_e - 1)
    live = step < s_end[n_e - 1]
    step_block = jnp.where(live, first_blk[step_expert] + step - s_start[step_expert], n_blocks - 1).astype(I32)
    row_lo = jnp.where(live, g_start[step_expert], 0).astype(I32)
    row_hi = jnp.where(live, g_end[step_expert], 0).astype(I32)
    experts = jnp.arange(n_e, dtype=I32)

    def slots(idx, pos):
        return pos + jnp.sum(jnp.where(idx[:, :, None] == experts, g_start, 0), axis=-1).astype(I32)

    dest_p, dest_s = slots(idx_p, pos_p), slots(idx_s, pos_s)
    xs_rows = _dispatch(jnp.concatenate([dest_p, dest_s], axis=1), h2_p, h2_s, n_assign, min(128, ts))
    y_rows = _experts(step_expert.astype(I32), step_block, row_lo, row_hi, xs_rows, w_exp_gate[0], w_exp_up[0],
                      w_exp_down[0])

    w_sg, w_su, w_sd = w_sh_gate[0].astype(BF16), w_sh_up[0].astype(BF16), w_sh_down[0].astype(BF16)
    y_p = _combine(dest_p, gate_p.T, h2_p, x1_p, w_sg, w_su, w_sd, mod_p, modo_p, True, seq, g_final, y_rows, 256)
    y_s = _combine(dest_s, gate_s.T, h2_s, x1_s, w_sg, w_su, w_sd, mod_s, modo_s, False, 1, g_final, y_rows, ts)

    kv = lambda z, b, t: z.reshape(1, b, t, heads, head_dim)
    kv_t = lambda z: z.reshape(1, bsz, heads, head_dim, seq).transpose(0, 1, 4, 2, 3)
    return (y_p.reshape(bsz, seq, d), y_s.reshape(dbsz, dseq, d),
            kv_t(k_p), kv_t(v_p), u_p3[:, seq - (width - 1):][None],
            kv(k_s, dbsz, dseq), kv(v_s, dbsz, dseq), u_ext_s[:, dseq:][None])
```

```python
import functools

import jax
import jax.numpy as jnp
import numpy as np
from jax import lax
from jax.experimental import pallas as pl
from jax.experimental.pallas import tpu as pltpu

F32 = jnp.float32
BF16 = jnp.bfloat16
I32 = jnp.int32

TOP_K = 6
N_EXPERT_GROUPS = 8
TOPK_GROUPS = 4
ROUTED_SCALE = 2.5
RMS_EPS = 1e-6
LN_EPS = 1e-5

LANES = 128
SUBLANES = 8
VMEM_BYTES_V7X = 64 * 1024 * 1024

ATTN_TILE = 256
MASKED_SCORE = -1e30
PAGES_PER_STEP = 8
EXPERT_ROWS = 256
CONV_HALO = 32
ROW_CHUNKS = 16
ROW_PITCH = 20


def _params(sem, vmem_mib):
    return pltpu.CompilerParams(dimension_semantics=sem, vmem_limit_bytes=vmem_mib * 1024 * 1024)


def _row(ref):
    return ref[0] if len(ref.shape) == 3 else ref[...]


def _sigmoid(x):
    return 1.0 / (1.0 + jnp.exp(-x))


def _softplus(z):
    return jnp.maximum(z, 0.0) + jnp.log(1.0 + jnp.exp(-jnp.abs(z)))


def _rms(x):
    return x * lax.rsqrt(jnp.mean(x * x, axis=-1, keepdims=True) + RMS_EPS)


def _mod_spec(per_seq, tm, d, chunk, tokens_per_seq):
    if per_seq:
        return pl.BlockSpec((1, 1, d), lambda i, *_: ((i * tm) // tokens_per_seq, 0, chunk))
    return pl.BlockSpec((tm, d), lambda i, *_: (i, chunk))


def _ada_kernel(c_ref, w_ref, b_ref, o_ref):
    c = c_ref[...]
    s = (c * _sigmoid(c)).astype(BF16)
    o_ref[...] = jnp.dot(s, w_ref[...].astype(BF16), preferred_element_type=F32) + b_ref[...]


def _ada(c, w, b, tn=512):
    m, d = c.shape
    n = w.shape[1]
    return pl.pallas_call(
        _ada_kernel,
        grid=(n // tn,),
        in_specs=[pl.BlockSpec((m, d), lambda j: (0, 0)),
                  pl.BlockSpec((d, tn), lambda j: (0, j)),
                  pl.BlockSpec((1, tn), lambda j: (0, j))],
        out_specs=pl.BlockSpec((m, tn), lambda j: (0, j)),
        out_shape=jax.ShapeDtypeStruct((m, n), F32),
        name="ada", compiler_params=_params(("arbitrary",), 32),
    )(c, w, b.reshape(1, n))


def _inproj_kernel(x_ref, sh_ref, sc_ref, g_ref, w_ref, q_ref, k_ref, v_ref, u_ref, h_scr, a_scr, *, q_scale,
                   feature_major_kv):
    j = pl.program_id(1)

    def put_kv(ref, p):
        if feature_major_kv:
            ref[0] = p.T
        else:
            ref[...] = p

    @pl.when(j == 0)
    def _():
        h = (_rms(x_ref[...]) * g_ref[...]) * (1.0 + _row(sc_ref)) + _row(sh_ref)
        h_scr[...] = h.astype(BF16)

    p = jnp.dot(h_scr[...], w_ref[...], preferred_element_type=F32)

    @pl.when(j == 0)
    def _():
        q_ref[...] = (p * q_scale).astype(BF16)

    @pl.when(j == 1)
    def _():
        put_kv(k_ref, p)

    @pl.when(j == 2)
    def _():
        put_kv(v_ref, p)

    @pl.when(j == 3)
    def _():
        a_scr[...] = p

    @pl.when(j == 4)
    def _():
        u_ref[...] = a_scr[...] * _sigmoid(p)


def _inproj(x, mod, per_seq, tokens_per_seq, g_mix, w_in_bf16, tm, q_scale):
    t, d = x.shape
    gw = w_in_bf16.shape[1] // 5
    tok = lambda i, j: (i, 0)
    out = jax.ShapeDtypeStruct((t, gw), F32)
    row_major = pl.BlockSpec((tm, gw), tok)
    if per_seq:
        per = tokens_per_seq // tm
        kv_spec = pl.BlockSpec((1, gw, tm), lambda i, j: (i // per, 0, i % per))
        kv_shape = jax.ShapeDtypeStruct((t // tokens_per_seq, gw, tokens_per_seq), F32)
    else:
        kv_spec, kv_shape = row_major, out
    return pl.pallas_call(
        functools.partial(_inproj_kernel, q_scale=q_scale, feature_major_kv=per_seq),
        grid=(t // tm, 5),
        in_specs=[pl.BlockSpec((tm, d), tok),
                  _mod_spec(per_seq, tm, d, 0, tokens_per_seq),
                  _mod_spec(per_seq, tm, d, 1, tokens_per_seq),
                  pl.BlockSpec((1, d), lambda i, j: (0, 0)),
                  pl.BlockSpec((d, gw), lambda i, j: (0, j))],
        out_specs=[row_major, kv_spec, kv_spec, row_major],
        out_shape=[jax.ShapeDtypeStruct((t, gw), BF16), kv_shape, kv_shape, out],
        scratch_shapes=[pltpu.VMEM((tm, d), BF16), pltpu.VMEM((tm, gw), F32)],
        name="inproj", compiler_params=_params(("arbitrary", "arbitrary"), 48),
    )(x, mod, mod, g_mix.reshape(1, d), w_in_bf16)


def _attn_kernel(bias_ref, sched_ref, q_ref, k_ref, v_ref, o_ref, kb, qm, vm, sp_scr, d_scr, bias_scr, *,
                 head_dim, tile):
    hp = pl.program_id(1)
    seq = q_ref.shape[0]
    first = lax.broadcasted_iota(I32, (1, LANES), 1) < head_dim
    first_rows = lax.broadcasted_iota(I32, (LANES, 1), 0) < head_dim
    q = q_ref[...]
    v = v_ref[0].astype(BF16)
    kb[...] = k_ref[0].astype(BF16)
    qm[0] = jnp.where(first, q, jnp.zeros_like(q))
    qm[1] = jnp.where(first, jnp.zeros_like(q), q)
    vm[0] = jnp.where(first_rows, v, jnp.zeros_like(v))
    vm[1] = jnp.where(first_rows, jnp.zeros_like(v), v)

    r = lax.broadcasted_iota(I32, (tile, tile), 0)
    c = lax.broadcasted_iota(I32, (tile, tile), 1)
    neg_later = jnp.concatenate([jnp.where(r > c, -1.0, 0.0), jnp.full((tile, LANES), -1.0, F32)],
                                axis=1).astype(BF16)
    causal = c < r

    nt = (((1,), (1,)), ((), ()))
    heads = (0, 1)
    for h in heads:
        bias = bias_ref[2 * hp + h]
        bias_scr[h, 0] = jnp.full((tile, tile), bias, F32)
        bias_scr[h, 1] = jnp.where(causal, bias, MASKED_SCORE)
    for scr in (sp_scr, d_scr):
        scr[...] = jnp.zeros(scr.shape, scr.dtype)

    n_iter = sched_ref.shape[1] - 1

    def step(t, slot, st):
        k_prev = pl.multiple_of(sched_ref[1, t], tile)
        first_prev = sched_ref[2, t] == 1
        q_next = pl.multiple_of(sched_ref[0, t + 1], tile)
        k_next = pl.multiple_of(sched_ref[1, t + 1], tile)
        diag_next = sched_ref[2, t + 1]
        later = [jnp.dot(sp_scr[h, slot], neg_later, preferred_element_type=F32) for h in heads]
        z = [jnp.dot(qm[h, pl.ds(q_next, tile), :], kb[:, pl.ds(k_next, tile)], preferred_element_type=F32)
             + bias_scr[h, diag_next] for h in heads]
        out = []
        for h in heads:
            acc = jnp.where(first_prev, 0.0, st[h][0])
            carry = jnp.where(first_prev, 0.0, st[h][1])
            wide = jnp.concatenate([carry] * (tile // LANES), axis=1)
            w = jnp.exp(d_scr[h, slot] + (later[h][:, :tile] + wide))
            acc = acc + lax.dot_general(w.astype(BF16), vm[h, :, pl.ds(k_prev, tile)], nt,
                                        preferred_element_type=F32)
            out.append((acc, carry + later[h][:, tile:]))
        for h in heads:
            sp = _softplus(z[h])
            sp_scr[h, 1 - slot] = sp.astype(BF16)
            d_scr[h, 1 - slot] = z[h] - sp

        @pl.when(sched_ref[3, t] == 1)
        def _():
            o_ref[pl.ds(pl.multiple_of(sched_ref[0, t], tile), tile), :] = out[0][0] + out[1][0]

        return tuple(out)

    def two_steps(k, st):
        return step(2 * k + 1, 1, step(2 * k, 0, st))

    init = tuple((jnp.zeros((tile, LANES), F32), jnp.zeros((tile, LANES), F32)) for _ in heads)
    lax.fori_loop(0, n_iter // 2, two_steps, init)


def _prompt_attention(q, k, v, b_sb, bsz, seq, head_dim):
    t, dsb = q.shape
    tile = ATTN_TILE
    idle = (0, 0, 0, 0)
    tiles = [(i * tile, j * tile, int(j == i), int(j == 0)) for i in range(seq // tile) for j in range(i, -1, -1)]
    entries = [idle] + tiles + [idle] * (1 + (len(tiles) + 1) % 2)
    sched = jnp.asarray(np.array(entries, np.int32).T)
    blk = pl.BlockSpec((seq, LANES), lambda b, hp: (b, hp))
    kv_blk = pl.BlockSpec((1, LANES, seq), lambda b, hp: (b, hp, 0))
    smem = pl.BlockSpec(memory_space=pltpu.SMEM)
    return pl.pallas_call(
        functools.partial(_attn_kernel, head_dim=head_dim, tile=tile),
        grid=(bsz, dsb // LANES),
        in_specs=[smem, smem, blk, kv_blk, kv_blk],
        out_specs=blk,
        out_shape=jax.ShapeDtypeStruct((t, dsb), F32),
        scratch_shapes=[pltpu.VMEM((LANES, seq), BF16), pltpu.VMEM((2, seq, LANES), BF16),
                        pltpu.VMEM((2, LANES, seq), BF16),
                        pltpu.VMEM((2, 2, tile, tile), BF16),
                        pltpu.VMEM((2, 2, tile, tile), F32),
                        pltpu.VMEM((2, 2, tile, tile), F32)],
        name="prompt_attn", compiler_params=_params(("arbitrary", "arbitrary"), 48),
    )(b_sb, sched, q, k, v)


def _sattn_kernel(pt_ref, qbd_ref, bias_ref, kn_ref, vn_ref, *rest, n_q, head_dim, page):
    npg = PAGES_PER_STEP
    k_refs, v_refs = rest[:npg], rest[npg:2 * npg]
    o_ref, acc_scr, carry_scr = rest[2 * npg:]
    step = pl.program_id(1)
    rows = qbd_ref.shape[1]
    qbd = qbd_ref[0]
    bias = bias_ref[...]
    r = lax.broadcasted_iota(I32, (page, page), 0)
    c = lax.broadcasted_iota(I32, (page, page), 1)
    neg_later = jnp.where(r > c, -1.0, 0.0).astype(BF16)

    def block(kp, vp, mask, acc, carry):
        z = jnp.dot(qbd, kp.astype(BF16), preferred_element_type=F32) + bias
        sp = _softplus(z)
        if mask is not None:
            sp = jnp.where(mask, sp, 0.0)
        later = jnp.dot(sp.astype(BF16), neg_later, preferred_element_type=F32) + carry
        w = jnp.exp(z - sp + later)
        if mask is not None:
            w = jnp.where(mask, w, 0.0)
        acc = acc + lax.dot_general(w.astype(BF16), vp.astype(BF16), (((1,), (1,)), ((), ())),
                                    preferred_element_type=F32)
        return acc, carry - jnp.sum(sp, axis=1, keepdims=True)

    @pl.when(step == 0)
    def _():
        qi = lax.broadcasted_iota(I32, (rows, page), 0) % n_q
        kj = lax.broadcasted_iota(I32, (rows, page), 1)
        acc, carry = block(kn_ref[0], vn_ref[0], kj < qi, jnp.zeros(acc_scr.shape, F32),
                           jnp.zeros((rows, 1), F32))
        acc_scr[...] = acc
        carry_scr[...] = carry

    order = list(reversed(range(npg)))
    zs = [jnp.dot(qbd, k_refs[j][0].astype(BF16), preferred_element_type=F32) + bias for j in order]
    sps = [_softplus(z) for z in zs]
    later_all = jnp.dot(jnp.concatenate([sp.astype(BF16) for sp in sps], axis=0), neg_later,
                        preferred_element_type=F32)
    carry = carry_scr[...]
    ws = []
    for n in range(npg):
        later = later_all[n * rows:(n + 1) * rows, :] + carry
        ws.append(jnp.exp(zs[n] - sps[n] + later).astype(BF16))
        carry = carry - jnp.sum(sps[n], axis=1, keepdims=True)
    acc = acc_scr[...]
    for n, j in enumerate(order):
        acc = acc + lax.dot_general(ws[n], v_refs[j][0].astype(BF16), (((1,), (1,)), ((), ())),
                                    preferred_element_type=F32)
    acc_scr[...] = acc
    carry_scr[...] = carry

    @pl.when(step == pl.num_programs(1) - 1)
    def _():
        width = acc_scr.shape[1]
        rh = lax.broadcasted_iota(I32, (rows, width), 0) // n_q
        ch = lax.broadcasted_iota(I32, (rows, width), 1) // head_dim
        own = jnp.where(rh == ch, acc, 0.0)
        folded = own[:, 0:LANES]
        for b in range(1, width // LANES):
            folded = folded + own[:, b * LANES:(b + 1) * LANES]
        o_ref[0] = folded


def _sample_attention(qbd, bias_col, k_new, v_new, cache_k, cache_v, page_table, n_q, head_dim):
    dbsz, rows, dsb = qbd.shape
    n_pages = page_table.shape[1]
    page = cache_k.shape[2]
    steps = n_pages // PAGES_PER_STEP

    def page_spec(j):
        return pl.BlockSpec((1, dsb, page),
                            lambda b, s, pt: (pt[b, (steps - 1 - s) * PAGES_PER_STEP + j], 0, 0))

    per_b = lambda b, s, pt: (b, 0, 0)
    grid_spec = pltpu.PrefetchScalarGridSpec(
        num_scalar_prefetch=1,
        grid=(dbsz, steps),
        in_specs=[pl.BlockSpec((1, rows, dsb), per_b),
                  pl.BlockSpec((rows, 1), lambda b, s, pt: (0, 0)),
                  pl.BlockSpec((1, dsb, page), per_b),
                  pl.BlockSpec((1, dsb, page), per_b)]
                 + [page_spec(j) for j in range(PAGES_PER_STEP)] * 2,
        out_specs=pl.BlockSpec((1, rows, LANES), per_b),
        scratch_shapes=[pltpu.VMEM((rows, dsb), F32), pltpu.VMEM((rows, 1), F32)],
    )
    return pl.pallas_call(
        functools.partial(_sattn_kernel, n_q=n_q, head_dim=head_dim, page=page),
        grid_spec=grid_spec,
        out_shape=jax.ShapeDtypeStruct((dbsz, rows, LANES), F32),
        name="sample_attn", compiler_params=_params(("arbitrary", "arbitrary"), 48),
    )(page_table, qbd, bias_col, k_new, v_new, *([cache_k] * PAGES_PER_STEP), *([cache_v] * PAGES_PER_STEP))


def _conv_kernel(halo_ref, u_ref, cw_ref, cb_ref, lg_ref, lb_ref, o_ref, ext, y_scr, *, width, zero_first, sub):
    tt, ch = u_ref.shape[1], u_ref.shape[2]
    halo = halo_ref[0]
    if zero_first:
        halo = jnp.where(pl.program_id(1) == 0, 0.0, halo)
    ext[0:CONV_HALO, :] = halo
    ext[CONV_HALO:CONV_HALO + tt, :] = u_ref[0]
    ext[CONV_HALO + tt:CONV_HALO + tt + SUBLANES, :] = jnp.zeros((SUBLANES, ch), F32)
    lead = CONV_HALO - (width - 1)

    def lane_block(cb, _):
        c0 = pl.multiple_of(cb * LANES, LANES)
        for ts in range(tt // sub):
            partial = [None] * SUBLANES
            for a in range((lead + width - 1) // SUBLANES + 1):
                rows = ext[pl.ds(ts * sub + SUBLANES * a, sub + SUBLANES), pl.ds(c0, LANES)]
                for r in range(SUBLANES):
                    w = SUBLANES * a + r - lead
                    if 0 <= w < width:
                        term = rows * cw_ref[pl.ds(w, 1), pl.ds(c0, LANES)]
                        partial[r] = term if partial[r] is None else partial[r] + term
            acc = partial[0][0:sub, :]
            for r in range(1, SUBLANES):
                acc = acc + partial[r][r:r + sub, :]
            y_scr[pl.ds(ts * sub, sub), pl.ds(c0, LANES)] = acc
        return 0

    lax.fori_loop(0, ch // LANES, lane_block, 0)
    y = y_scr[...] + cb_ref[...]
    mu = jnp.mean(y, axis=-1, keepdims=True)
    yc = y - mu
    var = jnp.mean(yc * yc, axis=-1, keepdims=True)
    yn = yc * lax.rsqrt(var + LN_EPS) * lg_ref[...] + lb_ref[...]
    o_ref[0] = (yn * _sigmoid(yn)).astype(BF16)


def _conv(halo_src, u3, conv_w, conv_b, ln_g, ln_b, tt, halo_from_u):
    b, t, ch = u3.shape
    width = conv_w.shape[0]
    if halo_from_u:
        halo_spec = pl.BlockSpec((1, CONV_HALO, ch),
                                 lambda bi, i: (bi, jnp.maximum(i * (tt // CONV_HALO) - 1, 0), 0))
    else:
        halo_spec = pl.BlockSpec((1, CONV_HALO, ch), lambda bi, i: (bi, 0, 0))
    vec = pl.BlockSpec((1, ch), lambda bi, i: (0, 0))
    return pl.pallas_call(
        functools.partial(_conv_kernel, width=width, zero_first=halo_from_u, sub=min(tt, 32)),
        grid=(b, t // tt),
        in_specs=[halo_spec,
                  pl.BlockSpec((1, tt, ch), lambda bi, i: (bi, i, 0)),
                  pl.BlockSpec((width, ch), lambda bi, i: (0, 0)),
                  vec, vec, vec],
        out_specs=pl.BlockSpec((1, tt, ch), lambda bi, i: (bi, i, 0)),
        out_shape=jax.ShapeDtypeStruct((b, t, ch), BF16),
        scratch_shapes=[pltpu.VMEM((CONV_HALO + tt + SUBLANES, ch), F32), pltpu.VMEM((tt, ch), F32)],
        name="conv", compiler_params=_params(("arbitrary", "arbitrary"), 32),
    )(halo_src, u3, conv_w, conv_b.reshape(1, ch), ln_g.reshape(1, ch), ln_b.reshape(1, ch))


def _outproj_kernel(osb_ref, ocv_ref, x_ref, gsb_ref, wsb_ref, wcv_ref, gate_ref, gffn_ref, shf_ref, scf_ref,
                    xo_ref, h2_ref):
    osb = (_rms(osb_ref[...]) * gsb_ref[...]).astype(BF16)
    y = jnp.dot(osb, wsb_ref[...], preferred_element_type=F32)
    y = y + jnp.dot(ocv_ref[...], wcv_ref[...], preferred_element_type=F32)
    xn = x_ref[...] + _row(gate_ref) * y
    xo_ref[...] = xn
    h2 = (_rms(xn) * gffn_ref[...]) * (1.0 + _row(scf_ref)) + _row(shf_ref)
    h2_ref[...] = h2.astype(BF16)


def _outproj(o_sb, o_cv, x, g_sb_out, w_sb, w_cv, mod, per_seq, tokens_per_seq, g_ffn, tm):
    t, d = x.shape
    dsb, dcv = o_sb.shape[1], o_cv.shape[1]
    tok = lambda i: (i, 0)
    fix = lambda i: (0, 0)
    return pl.pallas_call(
        _outproj_kernel,
        grid=(t // tm,),
        in_specs=[pl.BlockSpec((tm, dsb), tok), pl.BlockSpec((tm, dcv), tok), pl.BlockSpec((tm, d), tok),
                  pl.BlockSpec((1, dsb), fix), pl.BlockSpec((dsb, d), fix), pl.BlockSpec((dcv, d), fix),
                  _mod_spec(per_seq, tm, d, 2, tokens_per_seq),
                  pl.BlockSpec((1, d), fix),
                  _mod_spec(per_seq, tm, d, 3, tokens_per_seq),
                  _mod_spec(per_seq, tm, d, 4, tokens_per_seq)],
        out_specs=[pl.BlockSpec((tm, d), tok), pl.BlockSpec((tm, d), tok)],
        out_shape=[jax.ShapeDtypeStruct((t, d), F32), jax.ShapeDtypeStruct((t, d), BF16)],
        name="outproj", compiler_params=_params(("arbitrary",), 48),
    )(o_sb, o_cv, x, g_sb_out.reshape(1, dsb), w_sb, w_cv, mod, g_ffn.reshape(1, d), mod, mod)


def _route_kernel(h_ref, whi_ref, wlo_ref, br_ref, cin_ref, idx_ref, gate_ref, pos_ref, cnt_ref):
    n_e = whi_ref.shape[0]
    tm = h_ref.shape[0]
    gsz = n_e // N_EXPERT_GROUPS
    neg = -jnp.inf

    @pl.when(pl.program_id(0) == 0)
    def _():
        cnt_ref[...] = cin_ref[...]

    h = h_ref[...]
    nt = (((1,), (1,)), ((), ()))
    logits = (lax.dot_general(whi_ref[...], h, nt, preferred_element_type=F32)
              + lax.dot_general(wlo_ref[...], h, nt, preferred_element_type=F32))
    scores = _sigmoid(logits)
    biased = scores + br_ref[...]

    sub = lax.broadcasted_iota(I32, (gsz, tm), 0).astype(F32)

    def top1(x, iota, size):
        m = jnp.max(x, axis=0, keepdims=True)
        first = jnp.min(jnp.where(x == m, iota, float(size)), axis=0, keepdims=True)
        return m, iota == first

    gscore = []
    for g in range(N_EXPERT_GROUPS):
        xg = biased[g * gsz:(g + 1) * gsz, :]
        m1, hit = top1(xg, sub, gsz)
        m2 = jnp.max(jnp.where(hit, neg, xg), axis=0, keepdims=True)
        gscore.append(m1 + m2)

    chosen = [jnp.zeros((1, tm), jnp.bool_) for _ in range(N_EXPERT_GROUPS)]
    for _ in range(TOPK_GROUPS):
        m = functools.reduce(jnp.maximum, gscore)
        found = jnp.zeros((1, tm), jnp.bool_)
        for g in range(N_EXPERT_GROUPS):
            take = jnp.logical_and(gscore[g] == m, jnp.logical_not(found))
            found = jnp.logical_or(found, take)
            chosen[g] = jnp.logical_or(chosen[g], take)
            gscore[g] = jnp.where(take, neg, gscore[g])

    masked = jnp.concatenate(
        [jnp.where(chosen[g], biased[g * gsz:(g + 1) * gsz, :], neg) for g in range(N_EXPERT_GROUPS)], axis=0)

    eio = lax.broadcasted_iota(I32, (n_e, tm), 0).astype(F32)
    hits, raw = [], []
    for _ in range(TOP_K):
        _, hit = top1(masked, eio, n_e)
        hits.append(hit)
        raw.append(jnp.sum(jnp.where(hit, scores, 0.0), axis=0, keepdims=True))
        masked = jnp.where(hit, neg, masked)
    denom = functools.reduce(jnp.add, raw)

    sel = functools.reduce(jnp.logical_or, hits)
    sel_f = jnp.where(sel, 1.0, 0.0)
    tr = lax.broadcasted_iota(I32, (tm, tm), 0)
    tc = lax.broadcasted_iota(I32, (tm, tm), 1)
    before = jnp.where(tr < tc, 1.0, 0.0).astype(BF16)
    slot = jnp.dot(sel_f.astype(BF16), before, preferred_element_type=F32) + cnt_ref[:, 0:1]

    idx_ref[...] = jnp.zeros(idx_ref.shape, I32)
    gate_ref[...] = jnp.zeros(gate_ref.shape, F32)
    pos_ref[...] = jnp.zeros(pos_ref.shape, I32)
    for k in range(TOP_K):
        idx_ref[k:k + 1, :] = jnp.sum(jnp.where(hits[k], eio, 0.0), axis=0, keepdims=True).astype(I32)
        gate_ref[k:k + 1, :] = raw[k] / denom * ROUTED_SCALE
        pos_ref[k:k + 1, :] = jnp.sum(jnp.where(hits[k], slot, 0.0), axis=0, keepdims=True).astype(I32)
    cnt_ref[...] = cnt_ref[...] + jnp.sum(sel_f, axis=1, keepdims=True)


def _route(h2, wr_hi, wr_lo, b_router, counts_in, tm):
    t, d = h2.shape
    n_e = wr_hi.shape[0]
    fix = lambda i: (0, 0)
    per_tok = pl.BlockSpec((SUBLANES, tm), lambda i: (0, i))
    return pl.pallas_call(
        _route_kernel,
        grid=(t // tm,),
        in_specs=[pl.BlockSpec((tm, d), lambda i: (i, 0)),
                  pl.BlockSpec((n_e, d), fix), pl.BlockSpec((n_e, d), fix),
                  pl.BlockSpec((n_e, 1), fix), pl.BlockSpec((n_e, LANES), fix)],
        out_specs=[per_tok, per_tok, per_tok, pl.BlockSpec((n_e, LANES), fix)],
        out_shape=[jax.ShapeDtypeStruct((SUBLANES, t), I32), jax.ShapeDtypeStruct((SUBLANES, t), F32),
                   jax.ShapeDtypeStruct((SUBLANES, t), I32), jax.ShapeDtypeStruct((n_e, LANES), F32)],
        name="route", compiler_params=_params(("arbitrary",), 32),
    )(h2, wr_hi, wr_lo, b_router.reshape(n_e, 1), counts_in)


def _row_copy(src, dst, sem, src_row, dst_row):
    return pltpu.make_async_copy(src.at[pl.ds(src_row * ROW_PITCH, ROW_PITCH), :],
                                 dst.at[pl.ds(dst_row * ROW_PITCH, ROW_PITCH), :], sem)


def _dispatch_kernel(dest_ref, hp_ref, hs_ref, xs_ref, rows, sem, *, prompt_tiles):
    tm = hp_ref.shape[0]
    h = jnp.where(pl.program_id(0) < prompt_tiles, hp_ref[...], hs_ref[...]).astype(F32)
    for cidx in range(ROW_CHUNKS):
        rows[pl.ds(cidx, tm, stride=ROW_PITCH), :] = h[:, cidx * LANES:(cidx + 1) * LANES]
    for cidx in range(ROW_CHUNKS, ROW_PITCH):
        rows[pl.ds(cidx, tm, stride=ROW_PITCH), :] = jnp.zeros((tm, LANES), F32)

    def start(t, _):
        for k in range(TOP_K):
            _row_copy(rows, xs_ref, sem, t, dest_ref[k, t]).start()
        return 0

    def wait(t, _):
        for k in range(TOP_K):
            _row_copy(rows, xs_ref, sem, t, dest_ref[k, t]).wait()
        return 0

    lax.fori_loop(0, tm, start, 0)
    lax.fori_loop(0, tm, wait, 0)


def _dispatch(dest, h2_p, h2_s, n_rows, tm):
    tp, d = h2_p.shape
    ts = h2_s.shape[0]
    n_p, n_s = tp // tm, ts // tm
    return pl.pallas_call(
        functools.partial(_dispatch_kernel, prompt_tiles=n_p),
        grid=(n_p + n_s,),
        in_specs=[pl.BlockSpec((SUBLANES, tm), lambda i: (0, i), memory_space=pltpu.SMEM),
                  pl.BlockSpec((tm, d), lambda i: (jnp.minimum(i, n_p - 1), 0)),
                  pl.BlockSpec((tm, d), lambda i: (jnp.maximum(i - n_p, 0), 0))],
        out_specs=pl.BlockSpec(memory_space=pl.ANY),
        out_shape=jax.ShapeDtypeStruct((n_rows * ROW_PITCH, LANES), F32),
        scratch_shapes=[pltpu.VMEM((tm * ROW_PITCH, LANES), F32), pltpu.SemaphoreType.DMA(())],
        name="dispatch", compiler_params=_params(("arbitrary",), 32),
    )(dest, h2_p, h2_s)


def _expert_kernel(se_ref, sb_ref, lo_ref, hi_ref, xs_ref, wg_ref, wu_ref, wd_ref, y_ref, wg_b, wu_b, wd_b, acc):
    s = pl.program_id(0)
    prev = jnp.maximum(s - 1, 0)
    first = s == 0

    @pl.when(jnp.logical_or(first, se_ref[s] != se_ref[prev]))
    def _():
        wg_b[...] = wg_ref[0].astype(BF16)
        wu_b[...] = wu_ref[0].astype(BF16)
        wd_b[...] = wd_ref[0].astype(BF16)

    rb = xs_ref.shape[0] // ROW_PITCH
    x = jnp.concatenate([xs_ref[pl.ds(cidx, rb, stride=ROW_PITCH), :] for cidx in range(ROW_CHUNKS)],
                        axis=1).astype(BF16)
    g = jnp.dot(x, wg_b[...], preferred_element_type=F32)
    u = jnp.dot(x, wu_b[...], preferred_element_type=F32)
    a = ((g * _sigmoid(g)) * u).astype(BF16)
    y = jnp.dot(a, wd_b[...], preferred_element_type=F32)
    row = sb_ref[s] * rb + lax.broadcasted_iota(I32, (rb, 1), 0)
    y = jnp.where(jnp.logical_and(row >= lo_ref[s], row < hi_ref[s]), y, 0.0)
    new_block = jnp.logical_or(first, sb_ref[s] != sb_ref[prev])

    @pl.when(new_block)
    def _():
        acc[...] = y

    @pl.when(jnp.logical_not(new_block))
    def _():
        acc[...] = acc[...] + y

    for cidx in range(ROW_CHUNKS):
        y_ref[pl.ds(cidx, rb, stride=ROW_PITCH), :] = acc[:, cidx * LANES:(cidx + 1) * LANES]
    for cidx in range(ROW_CHUNKS, ROW_PITCH):
        y_ref[pl.ds(cidx, rb, stride=ROW_PITCH), :] = jnp.zeros((rb, LANES), F32)


def _experts(step_expert, step_block, row_lo, row_hi, xs, w_gate, w_up, w_down):
    n_steps = step_expert.shape[0]
    _, d, de = w_gate.shape
    rows = EXPERT_ROWS * ROW_PITCH
    by_expert = lambda s, se, sb, lo, hi: (se[s], 0, 0)
    by_block = lambda s, se, sb, lo, hi: (sb[s], 0)
    grid_spec = pltpu.PrefetchScalarGridSpec(
        num_scalar_prefetch=4,
        grid=(n_steps,),
        in_specs=[pl.BlockSpec((rows, LANES), by_block),
                  pl.BlockSpec((1, d, de), by_expert),
                  pl.BlockSpec((1, d, de), by_expert),
                  pl.BlockSpec((1, de, d), by_expert)],
        out_specs=pl.BlockSpec((rows, LANES), by_block),
        scratch_shapes=[pltpu.VMEM((d, de), BF16), pltpu.VMEM((d, de), BF16), pltpu.VMEM((de, d), BF16),
                        pltpu.VMEM((EXPERT_ROWS, d), F32)],
    )
    return pl.pallas_call(
        _expert_kernel,
        grid_spec=grid_spec,
        out_shape=jax.ShapeDtypeStruct(xs.shape, F32),
        name="experts", compiler_params=_params(("arbitrary",), 56),
    )(step_expert, step_block, row_lo, row_hi, xs, w_gate, w_up, w_down)


def _combine_kernel(dest_ref, gates_ref, h_ref, x_ref, wg_ref, wu_ref, wd_ref, gt_ref, gfin_ref, sh_ref, sc_ref,
                    y_hbm, o_ref, buf, sem):
    tm = h_ref.shape[0]

    def copy(t, k):
        return pltpu.make_async_copy(
            y_hbm.at[pl.ds(dest_ref[k, t] * ROW_PITCH, ROW_CHUNKS), :],
            buf.at[k, pl.ds(t * ROW_PITCH, ROW_CHUNKS), :], sem)

    def start(t, _):
        for k in range(TOP_K):
            copy(t, k).start()
        return 0

    def wait(t, _):
        for k in range(TOP_K):
            copy(t, k).wait()
        return 0

    lax.fori_loop(0, tm, start, 0)

    h = h_ref[...]
    g = jnp.dot(h, wg_ref[...], preferred_element_type=F32)
    u = jnp.dot(h, wu_ref[...], preferred_element_type=F32)
    shared = jnp.dot(((g * _sigmoid(g)) * u).astype(BF16), wd_ref[...], preferred_element_type=F32)

    lax.fori_loop(0, tm, wait, 0)

    gates = gates_ref[...]
    routed = jnp.zeros(shared.shape, F32)
    for k in range(TOP_K):
        yk = jnp.concatenate([buf[k, pl.ds(cidx, tm, stride=ROW_PITCH), :] for cidx in range(ROW_CHUNKS)], axis=1)
        routed = routed + yk * gates[:, k:k + 1]
    xo = x_ref[...] + _row(gt_ref) * (routed + shared)
    o_ref[...] = (_rms(xo) * gfin_ref[...]) * (1.0 + _row(sc_ref)) + _row(sh_ref)


def _combine(dest, gates_tok, h2, x, w_sg, w_su, w_sd, mod, mod_out, per_seq, tokens_per_seq, g_final, y_rows, tm):
    t, d = x.shape
    ds = w_sg.shape[1]
    tok = lambda i: (i, 0)
    fix = lambda i: (0, 0)
    return pl.pallas_call(
        _combine_kernel,
        grid=(t // tm,),
        in_specs=[pl.BlockSpec((SUBLANES, tm), lambda i: (0, i), memory_space=pltpu.SMEM),
                  pl.BlockSpec((tm, SUBLANES), tok),
                  pl.BlockSpec((tm, d), tok), pl.BlockSpec((tm, d), tok),
                  pl.BlockSpec((d, ds), fix), pl.BlockSpec((d, ds), fix), pl.BlockSpec((ds, d), fix),
                  _mod_spec(per_seq, tm, d, 5, tokens_per_seq),
                  pl.BlockSpec((1, d), fix),
                  _mod_spec(per_seq, tm, d, 0, tokens_per_seq),
                  _mod_spec(per_seq, tm, d, 1, tokens_per_seq),
                  pl.BlockSpec(memory_space=pl.ANY)],
        out_specs=pl.BlockSpec((tm, d), tok),
        out_shape=jax.ShapeDtypeStruct((t, d), F32),
        scratch_shapes=[pltpu.VMEM((TOP_K, tm * ROW_PITCH, LANES), F32), pltpu.SemaphoreType.DMA(())],
        name="combine", compiler_params=_params(("arbitrary",), 48),
    )(dest, gates_tok, h2, x, w_sg, w_su, w_sd, mod, g_final.reshape(1, d), mod_out, mod_out, y_rows)


def kernel(x_prompt, x_sample, cache_k, cache_v, state_conv, page_table, c_prompt, c_sample, w_ada, b_ada, g_mix, w_in, b_sb, conv_w, conv_b, conv_ln_g, conv_ln_b, g_sb_out, w_o, g_ffn, w_router, b_router, w_exp_gate, w_exp_up, w_exp_down, w_sh_gate, w_sh_up, w_sh_down, w_ada_out, b_ada_out, g_final):
    bsz, seq, d = x_prompt.shape
    dbsz, dseq, _ = x_sample.shape
    depth, n_pool, page, heads, head_dim = cache_k.shape
    assert depth == 1 and d == ROW_CHUNKS * LANES
    dsb = heads * head_dim
    dcv = conv_w.shape[-1]
    width = conv_w.shape[1]
    n_e = w_router.shape[-1]
    tp, ts = bsz * seq, dbsz * dseq
    assert dsb == dcv and w_in.shape[-1] == 5 * dsb and width - 1 <= CONV_HALO

    c_all = jnp.concatenate([c_prompt, c_sample], axis=0)
    pad = (-c_all.shape[0]) % SUBLANES
    c_all = jnp.pad(c_all, ((0, pad), (0, 0)))
    mod = _ada(c_all, w_ada[0], b_ada[0])
    mod_out = _ada(c_all, w_ada_out, b_ada_out)
    mod_p = mod[:bsz].reshape(bsz, 1, 6 * d)
    modo_p = mod_out[:bsz].reshape(bsz, 1, 2 * d)
    mod_s = jnp.repeat(mod[bsz:bsz + dbsz], dseq, axis=0)
    modo_s = jnp.repeat(mod_out[bsz:bsz + dbsz], dseq, axis=0)

    xp = x_prompt.reshape(tp, d)
    xs = x_sample.reshape(ts, d)
    w_in_b = w_in[0].astype(BF16)
    q_scale = head_dim ** -0.5

    q_p, k_p, v_p, u_p = _inproj(xp, mod_p, True, seq, g_mix[0], w_in_b, 512, q_scale)
    o_sb_p = _prompt_attention(q_p, k_p, v_p, b_sb[0], bsz, seq, head_dim)
    u_p3 = u_p.reshape(bsz, seq, dcv)
    o_cv_p = _conv(u_p3, u_p3, conv_w[0], conv_b[0], conv_ln_g[0], conv_ln_b[0], 256, True).reshape(tp, dcv)
    w_o_b = w_o[0].astype(BF16)
    x1_p, h2_p = _outproj(o_sb_p, o_cv_p, xp, g_sb_out[0], w_o_b[:dsb], w_o_b[dsb:], mod_p, True, seq, g_ffn[0], 256)

    q_s, k_s, v_s, u_s = _inproj(xs, mod_s, False, 1, g_mix[0], w_in_b, ts, q_scale)
    rows = heads * dseq
    eye = jnp.eye(heads, dtype=q_s.dtype)
    qbd = jnp.einsum('bqhd,hg->bhqgd', q_s.reshape(dbsz, dseq, heads, head_dim), eye).reshape(dbsz, rows, dsb)
    bias_col = jnp.repeat(b_sb[0], dseq).reshape(rows, 1)
    new_page = lambda z: jnp.pad(z.reshape(dbsz, dseq, dsb).transpose(0, 2, 1), ((0, 0), (0, 0), (0, page - dseq)))
    pages = lambda cache: cache[0].transpose(0, 2, 3, 1).reshape(n_pool, dsb, page)
    folded = _sample_attention(qbd, bias_col, new_page(k_s), new_page(v_s), pages(cache_k), pages(cache_v),
                               page_table, dseq, head_dim)
    folded = folded.reshape(dbsz, heads, dseq, 2, head_dim)
    o_sb_s = (folded[:, :, :, 0] + folded[:, :, :, 1]).transpose(0, 2, 1, 3).reshape(ts, dsb)
    u_s3 = u_s.reshape(dbsz, dseq, dcv)
    u_ext_s = jnp.concatenate([state_conv[0], u_s3], axis=1)
    halo_s = jnp.pad(state_conv[0], ((0, 0), (CONV_HALO - (width - 1), 0), (0, 0)))
    rows_s = 2 * SUBLANES
    u_s16 = jnp.pad(u_s3, ((0, 0), (0, rows_s - dseq), (0, 0)))
    o_cv_s = _conv(halo_s, u_s16, conv_w[0], conv_b[0], conv_ln_g[0], conv_ln_b[0], rows_s, False)
    o_cv_s = o_cv_s[:, :dseq].reshape(ts, dcv)
    x1_s, h2_s = _outproj(o_sb_s, o_cv_s, xs, g_sb_out[0], w_o_b[:dsb], w_o_b[dsb:], mod_s, False, 1, g_ffn[0], ts)

    wr_t = w_router[0].T
    wr_hi = wr_t.astype(BF16)
    wr_lo = (wr_t - wr_hi.astype(F32)).astype(BF16)
    zero_counts = jnp.zeros((n_e, LANES), F32)
    idx_p, gate_p, pos_p, counts = _route(h2_p, wr_hi, wr_lo, b_router[0], zero_counts, 512)
    idx_s, gate_s, pos_s, counts = _route(h2_s, wr_hi, wr_lo, b_router[0], counts, ts)

    n_assign = (tp + ts) * TOP_K
    assert n_assign % EXPERT_ROWS == 0
    n_blocks = n_assign // EXPERT_ROWS
    n_steps = n_blocks + n_e - 1
    cnt = counts[:, 0].astype(I32)
    g_end = jnp.cumsum(cnt)
    g_start = g_end - cnt
    first_blk = g_start // EXPERT_ROWS
    visits = jnp.where(cnt > 0, (g_end - 1) // EXPERT_ROWS - first_blk + 1, 0)
    s_end = jnp.cumsum(visits)
    s_start = s_end - visits
    step = jnp.arange(n_steps, dtype=I32)
    step_expert = jnp.minimum(jnp.sum((s_end[None, :] <= step[:, None]).astype(I32), axis=1), n_e - 1)
    live = step < s_end[n_e - 1]
    step_block = jnp.where(live, first_blk[step_expert] + step - s_start[step_expert], n_blocks - 1).astype(I32)
    row_lo = jnp.where(live, g_start[step_expert], 0).astype(I32)
    row_hi = jnp.where(live, g_end[step_expert], 0).astype(I32)
    experts = jnp.arange(n_e, dtype=I32)

    def slots(idx, pos):
        return pos + jnp.sum(jnp.where(idx[:, :, None] == experts, g_start, 0), axis=-1).astype(I32)

    dest_p, dest_s = slots(idx_p, pos_p), slots(idx_s, pos_s)
    xs_rows = _dispatch(jnp.concatenate([dest_p, dest_s], axis=1), h2_p, h2_s, n_assign, min(128, ts))
    y_rows = _experts(step_expert.astype(I32), step_block, row_lo, row_hi, xs_rows, w_exp_gate[0], w_exp_up[0],
                      w_exp_down[0])

    w_sg, w_su, w_sd = w_sh_gate[0].astype(BF16), w_sh_up[0].astype(BF16), w_sh_down[0].astype(BF16)
    y_p = _combine(dest_p, gate_p.T, h2_p, x1_p, w_sg, w_su, w_sd, mod_p, modo_p, True, seq, g_final, y_rows, 256)
    y_s = _combine(dest_s, gate_s.T, h2_s, x1_s, w_sg, w_su, w_sd, mod_s, modo_s, False, 1, g_final, y_rows, ts)

    kv = lambda z, b, t: z.reshape(1, b, t, heads, head_dim)
    kv_t = lambda z: z.reshape(1, bsz, heads, head_dim, seq).transpose(0, 1, 4, 2, 3)
    return (y_p.reshape(bsz, seq, d), y_s.reshape(dbsz, dseq, d),
            kv_t(k_p), kv_t(v_p), u_p3[:, seq - (width - 1):][None],
            kv(k_s, dbsz, dseq), kv(v_s, dbsz, dseq), u_ext_s[:, dseq:][None])
```
